```python
import jax, jax.numpy as jnp
from jax import lax
import numpy as np

D_MODEL = 4096
BATCH = 4
SEQ = 2048
DEPTH = 2
DEC_BATCH = 8
DEC_SEQ = 1
PAST_LEN = 16384
PAGE_SIZE = 128

N_A_LAYERS = DEPTH // 2
N_B_LAYERS = DEPTH - N_A_LAYERS
LRU_WIDTH = D_MODEL
LRU_HEAD_DIM = 256
N_LRU_HEADS = LRU_WIDTH // LRU_HEAD_DIM
CONV_WIDTH = 4
LRU_C = 8.0
HEAD_DIM = 128
N_HEADS = D_MODEL // HEAD_DIM
D_ATTN = N_HEADS * HEAD_DIM
MOBA_BLOCK = 256
MOBA_TOP_K = 3
Q_CHUNK = 16
D_FF = ((8 * D_MODEL + 767) // 768) * 256
RMS_EPS = 1e-6

kernel_name = "yoco_rglru_moba_decoder_step"


def rms_norm(x, g):
    xf = x.astype(jnp.float32)
    y = xf * lax.rsqrt(jnp.mean(xf * xf, axis=-1, keepdims=True) + RMS_EPS)
    return (y * g.astype(jnp.float32)).astype(x.dtype)


def swiglu(x, w_gate, w_up, w_down):
    return (jax.nn.silu(x @ w_gate) * (x @ w_up)) @ w_down


def causal_conv(x, buf, w, b):
    T = x.shape[1]
    xp = jnp.concatenate([buf.astype(x.dtype), x], axis=1)
    out = b + w[0] * xp[:, 0:T]
    for j in range(1, CONV_WIDTH):
        out = out + w[j] * xp[:, j:j + T]
    return out, xp[:, -(CONV_WIDTH - 1):]


def block_diag_linear(x, w, b):
    xh = x.reshape(x.shape[:-1] + (N_LRU_HEADS, LRU_HEAD_DIM))
    y = jnp.einsum('nthi,hij->nthj', xh, w) + b
    return y.reshape(x.shape)


def rg_lru(x, pos, h0, w_a, b_a, w_i, b_i, lam):
    xf = x.astype(jnp.float32)
    r = jax.nn.sigmoid(block_diag_linear(x, w_a, b_a).astype(jnp.float32))
    i = jax.nn.sigmoid(block_diag_linear(x, w_i, b_i).astype(jnp.float32))
    log_a = LRU_C * r * jax.nn.log_sigmoid(lam.astype(jnp.float32))
    a = jnp.exp(log_a)
    mult = jnp.sqrt(-jnp.expm1(2.0 * log_a))
    mult = jnp.where((pos == 0)[None, :, None], 1.0, mult)
    u = mult * i * xf

    def step(h, au):
        a_t, u_t = au
        h = a_t * h + u_t
        return h, h

    h_last, hs = lax.scan(step, h0.astype(jnp.float32), (a.swapaxes(0, 1), u.swapaxes(0, 1)))
    return hs.swapaxes(0, 1).astype(x.dtype), h_last.astype(x.dtype)


def recurrent_block(x, pos, conv_buf, h0, w_x, w_y, conv_w, conv_b, w_a, b_a, w_i, b_i, lam, w_out):
    gate = jax.nn.gelu(x @ w_y)
    xr = x @ w_x
    xc, new_buf = causal_conv(xr, conv_buf, conv_w, conv_b)
    h, h_last = rg_lru(xc, pos, h0, w_a, b_a, w_i, b_i, lam)
    return (h * gate) @ w_out, new_buf, h_last


def moba_attention(q, k, v, q_pos):
    N, T = q.shape[0], q.shape[1]
    n_blk = k.shape[1] // MOBA_BLOCK
    kb = k.reshape(N, n_blk, MOBA_BLOCK, N_HEADS, HEAD_DIM)
    vb = v.reshape(N, n_blk, MOBA_BLOCK, N_HEADS, HEAD_DIM)
    cur = (q_pos // MOBA_BLOCK).astype(jnp.int32)
    n_sel = min(MOBA_TOP_K, n_blk - 1)
    if n_sel > 0:
        k_mean = jnp.mean(kb.astype(jnp.float32), axis=2)
        gate = jnp.einsum('nthd,nbhd->nhtb', q.astype(jnp.float32), k_mean)
        past = jnp.arange(n_blk)[None, :] < cur[:, None]
        gate = jnp.where(past, gate, -jnp.inf)
        _, top_idx = lax.top_k(gate, n_sel)
        top_valid = jnp.arange(n_sel)[None, :] < cur[:, None]
    else:
        top_idx = jnp.zeros((N, N_HEADS, T, 0), jnp.int32)
        top_valid = jnp.zeros((T, 0), bool)
    own = jnp.broadcast_to(cur[None, None, :, None], (N, N_HEADS, T, 1))
    blk_idx = jnp.concatenate([top_idx.astype(jnp.int32), own], axis=-1)
    blk_valid = jnp.concatenate([top_valid, jnp.ones((T, 1), bool)], axis=-1)
    S = blk_idx.shape[-1]
    n_ar = jnp.arange(N)[:, None, None, None]
    h_ar = jnp.arange(N_HEADS)[None, :, None, None]
    offs = jnp.arange(MOBA_BLOCK, dtype=jnp.int32)
    scale = HEAD_DIM ** -0.5

    def attend(args):
        qc, idx, valid, pos = args
        qn = qc.shape[1]
        kg = kb[n_ar, idx, :, h_ar, :]
        s = jnp.einsum('nqhd,nhqsjd->nhqsj', qc, kg, preferred_element_type=jnp.float32) * scale
        key_pos = idx[..., None] * MOBA_BLOCK + offs
        mask = valid[None, None, :, :, None] & (key_pos <= pos[None, None, :, None, None])
        s = jnp.where(mask, s, -jnp.inf)
        p = jax.nn.softmax(s.reshape(N, N_HEADS, qn, S * MOBA_BLOCK), axis=-1).reshape(s.shape)
        vg = vb[n_ar, idx, :, h_ar, :]
        o = jnp.einsum('nhqsj,nhqsjd->nqhd', p.astype(vg.dtype), vg, preferred_element_type=jnp.float32)
        return o.astype(q.dtype)

    if T > Q_CHUNK and T % Q_CHUNK == 0:
        nc = T // Q_CHUNK
        qs = q.reshape(N, nc, Q_CHUNK, N_HEADS, HEAD_DIM).swapaxes(0, 1)
        idxs = blk_idx.reshape(N, N_HEADS, nc, Q_CHUNK, S).transpose(2, 0, 1, 3, 4)
        valids = blk_valid.reshape(nc, Q_CHUNK, S)
        poss = q_pos.reshape(nc, Q_CHUNK)
        out = lax.map(attend, (qs, idxs, valids, poss))
        return out.swapaxes(0, 1).reshape(N, T, N_HEADS, HEAD_DIM)
    return attend((q, blk_idx, blk_valid, q_pos))


def setup_inputs(seed: int = 0) -> dict:
    key = jax.random.key(seed)
    ks = iter(jax.random.split(key, 40))

    def nrm(shape, scale):
        return scale * jax.random.normal(next(ks), shape, jnp.float32)

    def gain(shape):
        return 1.0 + 0.02 * jax.random.normal(next(ks), shape, jnp.float32)

    n_pages = PAST_LEN // PAGE_SIZE
    n_used = DEC_BATCH * n_pages
    n_phys = n_used + max(1, n_used // 4)
    perm = jax.random.permutation(next(ks), n_phys)
    page_table = perm[:n_used].reshape(DEC_BATCH, n_pages).astype(jnp.int32)

    u = jax.random.uniform(next(ks), (N_A_LAYERS, LRU_WIDTH), jnp.float32, 0.9, 0.999)
    base = u ** (1.0 / LRU_C)
    lru_lambda = jnp.log(base) - jnp.log1p(-base)

    return {
        "x_prompt": nrm((BATCH, SEQ, D_MODEL), 1.0),
        "x_sample": nrm((DEC_BATCH, DEC_SEQ, D_MODEL), 1.0),
        "cache_k": nrm((n_phys, PAGE_SIZE, N_HEADS, HEAD_DIM), 1.0),
        "cache_v": nrm((n_phys, PAGE_SIZE, N_HEADS, HEAD_DIM), 1.0),
        "page_table": page_table,
        "state_conv": nrm((N_A_LAYERS, DEC_BATCH, CONV_WIDTH - 1, LRU_WIDTH), 1.0),
        "state_h": nrm((N_A_LAYERS, DEC_BATCH, LRU_WIDTH), 0.5),
        "mixer_norm": gain((DEPTH, D_MODEL)),
        "lru_w_x": nrm((N_A_LAYERS, D_MODEL, LRU_WIDTH), D_MODEL ** -0.5),
        "lru_w_y": nrm((N_A_LAYERS, D_MODEL, LRU_WIDTH), D_MODEL ** -0.5),
        "lru_conv_w": nrm((N_A_LAYERS, CONV_WIDTH, LRU_WIDTH), CONV_WIDTH ** -0.5),
        "lru_conv_b": nrm((N_A_LAYERS, LRU_WIDTH), 0.01),
        "lru_w_a": nrm((N_A_LAYERS, N_LRU_HEADS, LRU_HEAD_DIM, LRU_HEAD_DIM), LRU_HEAD_DIM ** -0.5),
        "lru_b_a": nrm((N_A_LAYERS, N_LRU_HEADS, LRU_HEAD_DIM), 0.01),
        "lru_w_i": nrm((N_A_LAYERS, N_LRU_HEADS, LRU_HEAD_DIM, LRU_HEAD_DIM), LRU_HEAD_DIM ** -0.5),
        "lru_b_i": nrm((N_A_LAYERS, N_LRU_HEADS, LRU_HEAD_DIM), 0.01),
        "lru_lambda": lru_lambda,
        "lru_w_out": nrm((N_A_LAYERS, LRU_WIDTH, D_MODEL), LRU_WIDTH ** -0.5),
        "kv_norm": gain((D_MODEL,)),
        "w_k": nrm((D_MODEL, D_ATTN), D_MODEL ** -0.5),
        "w_v": nrm((D_MODEL, D_ATTN), D_MODEL ** -0.5),
        "w_q": nrm((N_B_LAYERS, D_MODEL, D_ATTN), D_MODEL ** -0.5),
        "w_o": nrm((N_B_LAYERS, D_ATTN, D_MODEL), D_ATTN ** -0.5),
        "ffn_norm": gain((DEPTH, D_MODEL)),
        "w_gate": nrm((DEPTH, D_MODEL, D_FF), D_MODEL ** -0.5),
        "w_up": nrm((DEPTH, D_MODEL, D_FF), D_MODEL ** -0.5),
        "w_down": nrm((DEPTH, D_FF, D_MODEL), D_FF ** -0.5),
        "final_norm": gain((D_MODEL,)),
    }


def reference(x_prompt, x_sample, cache_k, cache_v, page_table, state_conv, state_h,
              mixer_norm, lru_w_x, lru_w_y, lru_conv_w, lru_conv_b, lru_w_a, lru_b_a,
              lru_w_i, lru_b_i, lru_lambda, lru_w_out, kv_norm, w_k, w_v, w_q, w_o,
              ffn_norm, w_gate, w_up, w_down, final_norm):
    bp, tp = x_prompt.shape[0], x_prompt.shape[1]
    bs, ts = x_sample.shape[0], x_sample.shape[1]
    past_len = page_table.shape[1] * cache_k.shape[1]
    pos_p = jnp.arange(tp, dtype=jnp.int32)
    pos_s = past_len + jnp.arange(ts, dtype=jnp.int32)
    hp, hs = x_prompt, x_sample
    zero_buf = jnp.zeros((bp, CONV_WIDTH - 1, LRU_WIDTH), x_prompt.dtype)
    zero_h = jnp.zeros((bp, LRU_WIDTH), jnp.float32)
    conv_p, conv_s, hl_p, hl_s = [], [], [], []

    for layer in range(DEPTH):
        if layer < N_A_LAYERS:
            a = layer
            w = (lru_w_x[a], lru_w_y[a], lru_conv_w[a], lru_conv_b[a], lru_w_a[a], lru_b_a[a],
                 lru_w_i[a], lru_b_i[a], lru_lambda[a], lru_w_out[a])
            mp, cbp, hlp = recurrent_block(rms_norm(hp, mixer_norm[layer]), pos_p, zero_buf, zero_h, *w)
            ms, cbs, hls = recurrent_block(rms_norm(hs, mixer_norm[layer]), pos_s, state_conv[a], state_h[a], *w)
            hp, hs = hp + mp, hs + ms
            conv_p.append(cbp); conv_s.append(cbs); hl_p.append(hlp); hl_s.append(hls)
        else:
            b = layer - N_A_LAYERS
            qp = (rms_norm(hp, mixer_norm[layer]) @ w_q[b]).reshape(bp, tp, N_HEADS, HEAD_DIM)
            qs = (rms_norm(hs, mixer_norm[layer]) @ w_q[b]).reshape(bs, ts, N_HEADS, HEAD_DIM)
            op = moba_attention(qp, k_ctx_p, v_ctx_p, pos_p)
            os_ = moba_attention(qs, k_ctx_s, v_ctx_s, pos_s)
            hp = hp + op.reshape(bp, tp, D_ATTN) @ w_o[b]
            hs = hs + os_.reshape(bs, ts, D_ATTN) @ w_o[b]
        hp = hp + swiglu(rms_norm(hp, ffn_norm[layer]), w_gate[layer], w_up[layer], w_down[layer])
        hs = hs + swiglu(rms_norm(hs, ffn_norm[layer]), w_gate[layer], w_up[layer], w_down[layer])

        if layer == N_A_LAYERS - 1:
            kvp = rms_norm(hp, kv_norm)
            k_prompt = (kvp @ w_k).reshape(bp, tp, N_HEADS, HEAD_DIM)
            v_prompt = (kvp @ w_v).reshape(bp, tp, N_HEADS, HEAD_DIM)
            pad_p = (-tp) % MOBA_BLOCK
            k_ctx_p = jnp.pad(k_prompt, ((0, 0), (0, pad_p), (0, 0), (0, 0)))
            v_ctx_p = jnp.pad(v_prompt, ((0, 0), (0, pad_p), (0, 0), (0, 0)))
            kvs = rms_norm(hs, kv_norm)
            k_sample = (kvs @ w_k).reshape(bs, ts, N_HEADS, HEAD_DIM)
            v_sample = (kvs @ w_v).reshape(bs, ts, N_HEADS, HEAD_DIM)
            pad_s = (-(past_len + ts)) % MOBA_BLOCK
            zpad = jnp.zeros((bs, pad_s, N_HEADS, HEAD_DIM), cache_k.dtype)
            k_ctx_s = jnp.concatenate([cache_k[page_table].reshape(bs, past_len, N_HEADS, HEAD_DIM),
                                       k_sample.astype(cache_k.dtype), zpad], axis=1)
            v_ctx_s = jnp.concatenate([cache_v[page_table].reshape(bs, past_len, N_HEADS, HEAD_DIM),
                                       v_sample.astype(cache_v.dtype), zpad], axis=1)

    y_prompt = rms_norm(hp, final_norm)
    y_sample = rms_norm(hs, final_norm)
    conv_prompt = jnp.stack(conv_p)
    conv_sample = jnp.stack(conv_s)
    h_prompt = jnp.stack(hl_p)
    h_sample = jnp.stack(hl_s)
    return (y_prompt, y_sample, k_prompt, v_prompt, k_sample, v_sample, conv_prompt, conv_sample, h_prompt, h_sample)
```

```python
import functools

import jax
import jax.numpy as jnp
from jax import lax
from jax.experimental import pallas as pl
from jax.experimental.pallas import tpu as pltpu

F32 = jnp.float32
BF16 = jnp.bfloat16

RMS_EPS = 1e-6
LRU_C = 8.0
MOBA_BLOCK = 256
MOBA_TOP_K = 3

V7X_VMEM_BYTES = 64 * 1024 * 1024
V7X_SUBLANES = 8
V7X_LANES = 128
_VMEM_INTERNAL_BYTES = 8 * 1024 * 1024


def _pick_tile(dim, target, align):
    best = None
    d = align
    while d <= min(dim, target):
        if dim % d == 0:
            best = d
        d += align
    return best if best is not None else dim


def _nbytes(shape, dtype):
    n = 1
    for s in shape:
        n *= s
    return n * jnp.dtype(dtype).itemsize


def _params(semantics, block_bytes, scratch_bytes=0):
    limit = 2 * block_bytes + scratch_bytes + _VMEM_INTERNAL_BYTES
    limit = min(max(limit, 16 * 1024 * 1024), V7X_VMEM_BYTES - 4 * 1024 * 1024)
    return pltpu.CompilerParams(dimension_semantics=semantics, vmem_limit_bytes=int(limit))


def _rmsnorm_kernel(x_ref, *refs, n_out):
    g_refs, o_refs = refs[:n_out], refs[n_out:]
    x = x_ref[...].astype(F32)
    y = x * lax.rsqrt(jnp.mean(x * x, axis=-1, keepdims=True) + RMS_EPS)
    for g_ref, o_ref in zip(g_refs, o_refs):
        o_ref[...] = (y * g_ref[...].astype(F32)).astype(o_ref.dtype)


def _rmsnorm(x, gains, out_dtypes, name):
    m, d = x.shape
    tm = _pick_tile(m, 256, V7X_SUBLANES)
    n_out = len(gains)
    block_bytes = _nbytes((tm, d), x.dtype) + sum(_nbytes((tm, d), dt) for dt in out_dtypes)
    outs = pl.pallas_call(
        functools.partial(_rmsnorm_kernel, n_out=n_out),
        grid=(m // tm,),
        in_specs=[pl.BlockSpec((tm, d), lambda i: (i, 0))]
        + [pl.BlockSpec((1, d), lambda i: (0, 0))] * n_out,
        out_specs=[pl.BlockSpec((tm, d), lambda i: (i, 0))] * n_out,
        out_shape=[jax.ShapeDtypeStruct((m, d), dt) for dt in out_dtypes],
        compiler_params=_params(("parallel",), block_bytes),
        name=name,
    )(x, *[g.reshape(1, d) for g in gains])
    return outs


def _matmul_kernel(*refs, n_w, n_extra, n_out, nk, epilogue):
    a_ref = refs[0]
    w_refs = refs[1:1 + n_w]
    e_refs = refs[1 + n_w:1 + n_w + n_extra]
    o_refs = refs[1 + n_w + n_extra:1 + n_w + n_extra + n_out]
    acc_refs = refs[1 + n_w + n_extra + n_out:]

    def finish(accs):
        outs = epilogue(*accs, *[e[...] for e in e_refs])
        for o_ref, val in zip(o_refs, outs):
            o_ref[...] = val.astype(o_ref.dtype)

    a = a_ref[...]
    if nk == 1:
        finish([jnp.dot(a, w[...], preferred_element_type=F32) for w in w_refs])
        return

    k = pl.program_id(2)

    @pl.when(k == 0)
    def _():
        for acc in acc_refs:
            acc[...] = jnp.zeros_like(acc)

    for w, acc in zip(w_refs, acc_refs):
        acc[...] += jnp.dot(a, w[...], preferred_element_type=F32)

    @pl.when(k == nk - 1)
    def _():
        finish([acc[...] for acc in acc_refs])


def _matmul(a, ws, epilogue, out_dtypes, extras=(), *, tm, tn, tk=None, name):
    m, kdim = a.shape
    n = ws[0].shape[1]
    tm = _pick_tile(m, tm, V7X_SUBLANES)
    tn = _pick_tile(n, tn, V7X_LANES)
    tk = kdim if tk is None else _pick_tile(kdim, tk, V7X_LANES)
    nk = kdim // tk
    n_w, n_extra, n_out = len(ws), len(extras), len(out_dtypes)
    block_bytes = (_nbytes((tm, tk), a.dtype) + sum(_nbytes((tk, tn), w.dtype) for w in ws)
                   + sum(_nbytes((tm, tn), e.dtype) for e in extras)
                   + sum(_nbytes((tm, tn), dt) for dt in out_dtypes))
    scratch = [pltpu.VMEM((tm, tn), F32) for _ in ws] if nk > 1 else []
    scratch_bytes = n_w * _nbytes((tm, tn), F32) if nk > 1 else 0
    return pl.pallas_call(
        functools.partial(_matmul_kernel, n_w=n_w, n_extra=n_extra, n_out=n_out, nk=nk,
                          epilogue=epilogue),
        grid=(m // tm, n // tn, nk),
        in_specs=[pl.BlockSpec((tm, tk), lambda i, j, k: (i, k))]
        + [pl.BlockSpec((tk, tn), lambda i, j, k: (k, j))] * n_w
        + [pl.BlockSpec((tm, tn), lambda i, j, k: (i, j))] * n_extra,
        out_specs=[pl.BlockSpec((tm, tn), lambda i, j, k: (i, j))] * n_out,
        out_shape=[jax.ShapeDtypeStruct((m, n), dt) for dt in out_dtypes],
        scratch_shapes=scratch,
        compiler_params=_params(("parallel", "parallel", "arbitrary"), block_bytes, scratch_bytes),
        name=name,
    )(a, *ws, *extras)


def _ep_identity(*accs):
    return accs


def _ep_x_gelu(acc_x, acc_y):
    return acc_x, jax.nn.gelu(acc_y)


def _ep_residual(acc, res):
    return (res + acc,)


def _ep_swiglu(acc_gate, acc_up):
    return (jax.nn.silu(acc_gate) * acc_up,)


def _lru_gates(xc, head, wa_ref, ba_ref, wi_ref, bi_ref, lam):
    xb = xc.astype(BF16)
    r = jax.nn.sigmoid(jnp.dot(xb, wa_ref[head], preferred_element_type=F32) + ba_ref[head])
    i = jax.nn.sigmoid(jnp.dot(xb, wi_ref[head], preferred_element_type=F32) + bi_ref[head])
    log_a = LRU_C * r * jax.nn.log_sigmoid(lam)
    a = jnp.exp(log_a)
    mult = jnp.sqrt(-jnp.tanh(log_a) * (a * a + 1.0))
    return a, mult, i


def _rglru_seq_kernel(xr_ref, gate_ref, cbuf_ref, h0_ref, cw_ref, cb_ref, wa_ref, ba_ref, wi_ref,
                      bi_ref, lam_ref, hg_ref, hlast_ref, xp_scr, a_scr, u_scr, hs_scr, h_scr, *,
                      tt, nh, dh, cwid, first_pos_zero):
    t = pl.program_id(2)
    halo = cwid - 1
    base = V7X_SUBLANES - halo

    @pl.when(t == 0)
    def _():
        xp_scr[base:V7X_SUBLANES, :] = cbuf_ref[0].astype(F32)
        h_scr[...] = h0_ref[0].astype(F32)

    xp_scr[V7X_SUBLANES:V7X_SUBLANES + tt, :] = xr_ref[0]
    xc = cb_ref[...] + cw_ref[0:1, :] * xp_scr[base:base + tt, :]
    for j in range(1, cwid):
        xc = xc + cw_ref[j:j + 1, :] * xp_scr[base + j:base + j + tt, :]
    xp_scr[base:V7X_SUBLANES, :] = xp_scr[base + tt:V7X_SUBLANES + tt, :]

    for hd in range(nh):
        sl = slice(hd * dh, (hd + 1) * dh)
        xh = xc[:, sl]
        a, mult, i = _lru_gates(xh, hd, wa_ref, ba_ref, wi_ref, bi_ref, lam_ref[:, sl])
        if first_pos_zero:
            row = lax.broadcasted_iota(jnp.int32, mult.shape, 0)
            mult = jnp.where((row == 0) & (t == 0), 1.0, mult)
        a_scr[:, sl] = a
        u_scr[:, sl] = mult * i * xh

    def step(s, h):
        h = a_scr[pl.ds(s, 1), :] * h + u_scr[pl.ds(s, 1), :]
        hs_scr[pl.ds(s, 1), :] = h
        return h

    h = lax.fori_loop(0, tt, step, h_scr[...], unroll=8)
    h_scr[...] = h
    hg_ref[0] = (hs_scr[...] * gate_ref[0]).astype(hg_ref.dtype)
    hlast_ref[0] = h


def _rglru_seq(xr, gate, conv_buf, h0, conv_w, conv_b, w_a, b_a, w_i, b_i, lam, *, first_pos_zero):
    b, t, w = xr.shape
    nheads, dh = w_a.shape[0], w_a.shape[1]
    cwid = conv_w.shape[0]
    wc = _pick_tile(w, 1024, dh)
    nh = wc // dh
    tt = _pick_tile(t, 256, V7X_SUBLANES)
    blk = lambda bi_, ci, ti: (bi_, ti, ci)
    per_bc = lambda bi_, ci, ti: (bi_, 0, ci)
    chan = lambda bi_, ci, ti: (0, ci)
    head = lambda bi_, ci, ti: (ci, 0, 0)
    block_bytes = (2 * _nbytes((tt, wc), F32) + _nbytes((tt, wc), BF16)
                   + 2 * _nbytes((nh, dh, dh), BF16) + 16 * _nbytes((1, wc), F32))
    scratch_bytes = 4 * _nbytes((tt + V7X_SUBLANES, wc), F32)
    hg, hlast = pl.pallas_call(
        functools.partial(_rglru_seq_kernel, tt=tt, nh=nh, dh=dh, cwid=cwid,
                          first_pos_zero=first_pos_zero),
        grid=(b, w // wc, t // tt),
        in_specs=[
            pl.BlockSpec((1, tt, wc), blk),
            pl.BlockSpec((1, tt, wc), blk),
            pl.BlockSpec((1, cwid - 1, wc), per_bc),
            pl.BlockSpec((1, 1, wc), per_bc),
            pl.BlockSpec((cwid, wc), chan),
            pl.BlockSpec((1, wc), chan),
            pl.BlockSpec((nh, dh, dh), head),
            pl.BlockSpec((nh, 1, dh), head),
            pl.BlockSpec((nh, dh, dh), head),
            pl.BlockSpec((nh, 1, dh), head),
            pl.BlockSpec((1, wc), chan),
        ],
        out_specs=[pl.BlockSpec((1, tt, wc), blk), pl.BlockSpec((1, 1, wc), per_bc)],
        out_shape=[jax.ShapeDtypeStruct((b, t, w), BF16), jax.ShapeDtypeStruct((b, 1, w), F32)],
        scratch_shapes=[
            pltpu.VMEM((tt + V7X_SUBLANES, wc), F32),
            pltpu.VMEM((tt, wc), F32),
            pltpu.VMEM((tt, wc), F32),
            pltpu.VMEM((tt, wc), F32),
            pltpu.VMEM((1, wc), F32),
        ],
        compiler_params=_params(("parallel", "parallel", "arbitrary"), block_bytes, scratch_bytes),
        name="rglru_seq",
    )(xr, gate, conv_buf, h0.reshape(b, 1, w), conv_w, conv_b.reshape(1, w),
      w_a.astype(BF16), b_a.reshape(nheads, 1, dh), w_i.astype(BF16), b_i.reshape(nheads, 1, dh),
      lam.reshape(1, w))
    return hg, hlast.reshape(b, w)


def _rglru_step_kernel(xr_ref, gate_ref, cbuf_ref, h0_ref, cw_ref, cb_ref, wa_ref, ba_ref, wi_ref,
                       bi_ref, lam_ref, hg_ref, h_ref, nbuf_ref, *, nheads, dh, cwid,
                       first_pos_zero):
    xr = xr_ref[...]
    xc = cb_ref[...] + cw_ref[cwid - 1:cwid, :] * xr
    for j in range(cwid - 1):
        xc = xc + cw_ref[j:j + 1, :] * cbuf_ref[j]
    for j in range(cwid - 2):
        nbuf_ref[j] = cbuf_ref[j + 1]
    nbuf_ref[cwid - 2] = xr
    for hd in range(nheads):
        sl = slice(hd * dh, (hd + 1) * dh)
        xh = xc[:, sl]
        a, mult, i = _lru_gates(xh, hd, wa_ref, ba_ref, wi_ref, bi_ref, lam_ref[:, sl])
        if first_pos_zero:
            mult = jnp.ones_like(mult)
        h = a * h0_ref[:, sl] + mult * i * xh
        h_ref[:, sl] = h
        hg_ref[:, sl] = (h * gate_ref[:, sl]).astype(hg_ref.dtype)


def _rglru_step(xr, gate, conv_buf, h0, conv_w, conv_b, w_a, b_a, w_i, b_i, lam, *, first_pos_zero):
    s, w = xr.shape
    nheads, dh = w_a.shape[0], w_a.shape[1]
    cwid = conv_w.shape[0]
    hg, h, nbuf = pl.pallas_call(
        functools.partial(_rglru_step_kernel, nheads=nheads, dh=dh, cwid=cwid,
                          first_pos_zero=first_pos_zero),
        out_shape=[jax.ShapeDtypeStruct((s, w), BF16), jax.ShapeDtypeStruct((s, w), F32),
                   jax.ShapeDtypeStruct((cwid - 1, s, w), F32)],
        name="rglru_step",
    )(xr, gate, jnp.swapaxes(conv_buf, 0, 1), h0, conv_w, conv_b.reshape(1, w),
      w_a.astype(BF16), b_a.reshape(nheads, 1, dh), w_i.astype(BF16), b_i.reshape(nheads, 1, dh),
      lam.reshape(1, w))
    return hg, h, jnp.swapaxes(nbuf, 0, 1)


def _moba_seq_kernel(q_ref, k_ref, v_ref, o_ref, *, nblk, blk, n_sel, scale):
    k32 = k_ref[0]
    kb = k32.astype(BF16)
    v_t = v_ref[0].T.astype(BF16)
    nblk_pad = -(-nblk // V7X_SUBLANES) * V7X_SUBLANES
    means = [jnp.mean(k32[b * blk:(b + 1) * blk], axis=0, keepdims=True) for b in range(nblk)]
    means += [jnp.zeros_like(means[0])] * (nblk_pad - nblk)
    kmean = jnp.concatenate(means, axis=0)
    key_i = lax.broadcasted_iota(jnp.int32, (blk, blk), 0)
    qry_i = lax.broadcasted_iota(jnp.int32, (blk, blk), 1)
    causal = key_i <= qry_i
    neg_inf = jnp.float32(-jnp.inf)
    contract_last = (((1,), (1,)), ((), ()))

    for c in range(nblk):
        q32 = q_ref[0, c * blk:(c + 1) * blk, :]
        qb = q32.astype(BF16)
        sels = [None] * c
        if c > n_sel:
            gate = lax.dot_general(kmean, q32, contract_last, precision=lax.Precision.HIGHEST,
                                   preferred_element_type=F32)
            g = [gate[b:b + 1, :] for b in range(c)]
            for b in range(c):
                cnt = jnp.zeros((1, blk), jnp.int32)
                for b2 in range(c):
                    if b2 == b:
                        continue
                    beats = (g[b2] >= g[b]) if b2 < b else (g[b2] > g[b])
                    cnt = cnt + beats.astype(jnp.int32)
                sels[b] = cnt < n_sel
        s_blocks = []
        for b in range(c + 1):
            s = lax.dot_general(kb[b * blk:(b + 1) * blk], qb, contract_last,
                                preferred_element_type=F32) * scale
            if b == c:
                s = jnp.where(causal, s, neg_inf)
            elif sels[b] is not None:
                s = jnp.where(sels[b], s, neg_inf)
            s_blocks.append(s)
        m = jnp.max(s_blocks[c], axis=0, keepdims=True)
        for b in range(c):
            m = jnp.maximum(m, jnp.max(s_blocks[b], axis=0, keepdims=True))
        l = jnp.zeros((1, blk), F32)
        o_t = jnp.zeros((v_t.shape[0], blk), F32)
        for b in range(c + 1):
            p = jnp.exp(s_blocks[b] - m)
            l = l + jnp.sum(p, axis=0, keepdims=True)
            o_t = o_t + jnp.dot(v_t[:, b * blk:(b + 1) * blk], p.astype(BF16),
                                preferred_element_type=F32)
        o_ref[0, c * blk:(c + 1) * blk, :] = (o_t / l).T.astype(o_ref.dtype)


def _moba_seq(q, k, v, n_heads):
    b, t, d = q.shape
    hd = d // n_heads
    assert t % MOBA_BLOCK == 0 and hd % V7X_LANES == 0
    nblk = t // MOBA_BLOCK
    spec = pl.BlockSpec((1, t, hd), lambda bi, hi: (bi, 0, hi))
    block_bytes = 3 * _nbytes((t, hd), F32) + _nbytes((t, hd), BF16)
    return pl.pallas_call(
        functools.partial(_moba_seq_kernel, nblk=nblk, blk=MOBA_BLOCK,
                          n_sel=min(MOBA_TOP_K, nblk - 1), scale=hd ** -0.5),
        grid=(b, n_heads),
        in_specs=[spec, spec, spec],
        out_specs=spec,
        out_shape=jax.ShapeDtypeStruct((b, t, d), BF16),
        compiler_params=_params(("parallel", "parallel"), block_bytes),
        name="moba_seq",
    )(q, k, v)


def _page_mean_kernel(pt_ref, *refs, npp, blk):
    page_refs, o_ref = refs[:npp], refs[npp]
    acc = jnp.sum(page_refs[0][0].astype(F32), axis=0)
    for p_ref in page_refs[1:]:
        acc = acc + jnp.sum(p_ref[0].astype(F32), axis=0)
    o_ref[0, 0] = acc / blk


def _page_block_means(cache_k, page_table):
    _, page, n_heads, hd = cache_k.shape
    s, npages = page_table.shape
    npp = MOBA_BLOCK // page
    nb = npages // npp

    def page_spec(i):
        return pl.BlockSpec((1, page, n_heads, hd),
                            lambda si, bi, pt: (pt[si * npages + bi * npp + i], 0, 0, 0))

    grid_spec = pltpu.PrefetchScalarGridSpec(
        num_scalar_prefetch=1,
        grid=(s, nb),
        in_specs=[page_spec(i) for i in range(npp)],
        out_specs=pl.BlockSpec((1, 1, n_heads, hd), lambda si, bi, pt: (si, bi, 0, 0)),
    )
    block_bytes = npp * _nbytes((page, n_heads, hd), cache_k.dtype) + _nbytes((n_heads, hd), F32)
    return pl.pallas_call(
        functools.partial(_page_mean_kernel, npp=npp, blk=MOBA_BLOCK),
        grid_spec=grid_spec,
        out_shape=jax.ShapeDtypeStruct((s, nb, n_heads, hd), F32),
        compiler_params=_params(("parallel", "arbitrary"), block_bytes),
        name="page_block_means",
    )(page_table.reshape(-1), *([cache_k] * npp))


def _block_select_kernel(q_ref, km_ref, sel_ref, *, n_sel):
    q = q_ref[0]
    km = km_ref[0]
    nb = km.shape[0]
    g = jnp.sum(km * q[None], axis=-1)
    blk_i = lax.broadcasted_iota(jnp.int32, g.shape, 0)
    rows = []
    for _ in range(n_sel):
        m = jnp.max(g, axis=0, keepdims=True)
        idx = jnp.min(jnp.where(g == m, blk_i, nb), axis=0, keepdims=True)
        rows.append(idx)
        g = jnp.where(blk_i == idx, -jnp.inf, g)
    sel_ref[0] = jnp.concatenate(rows, axis=0)


def _block_select(q, kmean, n_sel):
    s, nb, n_heads, hd = kmean.shape
    return pl.pallas_call(
        functools.partial(_block_select_kernel, n_sel=n_sel),
        grid=(s,),
        in_specs=[pl.BlockSpec((1, n_heads, hd), lambda si: (si, 0, 0)),
                  pl.BlockSpec((1, nb, n_heads, hd), lambda si: (si, 0, 0, 0))],
        out_specs=pl.BlockSpec((1, n_sel, n_heads), lambda si: (si, 0, 0)),
        out_shape=jax.ShapeDtypeStruct((s, n_sel, n_heads), jnp.int32),
        name="block_select",
    )(q, kmean)


def _moba_step_kernel(pt_ref, sel_ref, q_ref, ks_ref, vs_ref, *refs, n_kv, hpg, scale):
    k_refs, v_refs, o_ref = refs[:n_kv], refs[n_kv:2 * n_kv], refs[2 * n_kv]
    hh = pl.program_id(1) % hpg
    q = q_ref[0]
    hd = q.shape[-1]
    rows = 2 * V7X_SUBLANES
    qb = jnp.broadcast_to(q, (rows, hd)).astype(BF16)
    page = k_refs[0].shape[1]
    lane = lax.broadcasted_iota(jnp.int32, (1, page * hpg), 1)
    mine = (lane & (hpg - 1)) == hh
    contract_last = (((1,), (1,)), ((), ()))
    scores = []
    for k_ref in k_refs:
        k2 = k_ref[0].reshape(page * hpg, hd).astype(BF16)
        s = lax.dot_general(qb, k2, contract_last, preferred_element_type=F32)[0:1] * scale
        scores.append(jnp.where(mine, s, -jnp.inf))
    s_own = jnp.sum(q * ks_ref[0], axis=-1, keepdims=True) * scale
    m = s_own
    for s in scores:
        m = jnp.maximum(m, jnp.max(s, axis=-1, keepdims=True))
    p_own = jnp.exp(s_own - m)
    l = p_own
    o = p_own * vs_ref[0]
    for s, v_ref in zip(scores, v_refs):
        p = jnp.exp(s - m)
        l = l + jnp.sum(p, axis=-1, keepdims=True)
        v2 = v_ref[0].reshape(page * hpg, hd).astype(BF16)
        pb = jnp.broadcast_to(p, (rows, page * hpg)).astype(BF16)
        o = o + jnp.dot(pb, v2, preferred_element_type=F32)[0:1]
    o_ref[0] = (o / l).astype(o_ref.dtype)


def _moba_step(q, k_new, v_new, cache_k, cache_v, page_table, sel):
    s, n_heads, hd = q.shape
    _, page, _, _ = cache_k.shape
    npages = page_table.shape[1]
    n_sel = sel.shape[1]
    npp = MOBA_BLOCK // page
    hpg = V7X_SUBLANES
    assert n_heads % hpg == 0 and hpg & (hpg - 1) == 0
    n_kv = n_sel * npp
    row = lambda si, hi, pt, sl: (si * n_heads + hi, 0, 0)

    def page_spec(r, p):
        def index(si, hi, pt, sl):
            blk = sl[(si * n_sel + r) * n_heads + hi]
            return (pt[si * npages + blk * npp + p], 0, hi // hpg, 0)
        return pl.BlockSpec((1, page, hpg, hd), index)

    kv_specs = [page_spec(r, p) for r in range(n_sel) for p in range(npp)]
    grid_spec = pltpu.PrefetchScalarGridSpec(
        num_scalar_prefetch=2,
        grid=(s, n_heads),
        in_specs=[pl.BlockSpec((1, 1, hd), row)] * 3 + kv_specs + kv_specs,
        out_specs=pl.BlockSpec((1, 1, hd), row),
    )
    block_bytes = 2 * n_kv * _nbytes((page, hpg, hd), cache_k.dtype) + 4 * _nbytes((1, hd), F32)
    flat = lambda x: x.reshape(s * n_heads, 1, hd)
    out = pl.pallas_call(
        functools.partial(_moba_step_kernel, n_kv=n_kv, hpg=hpg, scale=hd ** -0.5),
        grid_spec=grid_spec,
        out_shape=jax.ShapeDtypeStruct((s * n_heads, 1, hd), F32),
        compiler_params=_params(("parallel", "parallel"), block_bytes),
        name="moba_step",
    )(page_table.reshape(-1), sel.reshape(-1), flat(q), flat(k_new), flat(v_new),
      *([cache_k] * n_kv), *([cache_v] * n_kv))
    return out.reshape(s, n_heads * hd)


def _swiglu_block(h, norm_gain, w_gate, w_up, w_down, tag, *, tm):
    (xn,) = _rmsnorm(h, [norm_gain], [BF16], f"ffn_norm_{tag}")
    (act,) = _matmul(xn, [w_gate, w_up], _ep_swiglu, [BF16], tm=tm, tn=256, name=f"ffn_gate_up_{tag}")
    (out,) = _matmul(act, [w_down], _ep_residual, [F32], [h], tm=min(tm, 512), tn=256,
                     name=f"ffn_down_{tag}")
    return out


def kernel(x_prompt, x_sample, cache_k, cache_v, page_table, state_conv, state_h, mixer_norm, lru_w_x, lru_w_y, lru_conv_w, lru_conv_b, lru_w_a, lru_b_a, lru_w_i, lru_b_i, lru_lambda, lru_w_out, kv_norm, w_k, w_v, w_q, w_o, ffn_norm, w_gate, w_up, w_down, final_norm):
    bp, tp, d = x_prompt.shape
    bs, ts, _ = x_sample.shape
    n_heads, hd = cache_k.shape[2], cache_k.shape[3]
    page = cache_k.shape[1]
    past_len = page_table.shape[1] * page
    depth = mixer_norm.shape[0]
    n_a = lru_w_x.shape[0]
    assert ts == 1, "the decode group advances one position per step"
    assert MOBA_BLOCK % page == 0 and past_len % MOBA_BLOCK == 0
    n_sel_s = min(MOBA_TOP_K, past_len // MOBA_BLOCK)
    assert past_len // MOBA_BLOCK >= n_sel_s

    bf = lambda w: w.astype(BF16)
    hp = x_prompt.reshape(bp * tp, d)
    hs = x_sample.reshape(bs * ts, d)
    tm_p, tm_s = 1024, 8
    conv_p, conv_s, hl_p, hl_s = [], [], [], []
    k_p = v_p = k_s = v_s = None

    for layer in range(depth):
        if layer < n_a:
            a = layer
            w_x, w_y, w_out = bf(lru_w_x[a]), bf(lru_w_y[a]), bf(lru_w_out[a])
            lru = (lru_conv_w[a], lru_conv_b[a], lru_w_a[a], lru_b_a[a], lru_w_i[a], lru_b_i[a],
                   lru_lambda[a])
            cwid = lru_conv_w.shape[1]
            (xn,) = _rmsnorm(hp, [mixer_norm[layer]], [BF16], f"mixer_norm_p{layer}")
            xr, gate = _matmul(xn, [w_x, w_y], _ep_x_gelu, [F32, F32], tm=tm_p, tn=256,
                               name=f"lru_in_p{layer}")
            w = xr.shape[1]
            xr3 = xr.reshape(bp, tp, w)
            hg, hlast = _rglru_seq(xr3, gate.reshape(bp, tp, w), jnp.zeros((bp, cwid - 1, w), F32),
                                   jnp.zeros((bp, w), F32), *lru, first_pos_zero=True)
            (hp,) = _matmul(hg.reshape(bp * tp, w), [w_out], _ep_residual, [F32], [hp], tm=tm_p,
                            tn=256, name=f"lru_out_p{layer}")
            conv_p.append(xr3[:, tp - (cwid - 1):, :])
            hl_p.append(hlast)
            (xn,) = _rmsnorm(hs, [mixer_norm[layer]], [BF16], f"mixer_norm_s{layer}")
            xr, gate = _matmul(xn, [w_x, w_y], _ep_x_gelu, [F32, F32], tm=tm_s, tn=1024,
                               name=f"lru_in_s{layer}")
            hg, hlast, nbuf = _rglru_step(xr, gate, state_conv[a], state_h[a], *lru,
                                          first_pos_zero=(past_len == 0))
            (hs,) = _matmul(hg, [w_out], _ep_residual, [F32], [hs], tm=tm_s, tn=1024,
                            name=f"lru_out_s{layer}")
            conv_s.append(nbuf)
            hl_s.append(hlast)
        else:
            b = layer - n_a
            wq, wo = bf(w_q[b]), bf(w_o[b])
            (xn,) = _rmsnorm(hp, [mixer_norm[layer]], [BF16], f"mixer_norm_p{layer}")
            (q,) = _matmul(xn, [wq], _ep_identity, [F32], tm=tm_p, tn=256, name=f"attn_q_p{layer}")
            o = _moba_seq(q.reshape(bp, tp, d), k_p.reshape(bp, tp, d), v_p.reshape(bp, tp, d),
                          n_heads)
            (hp,) = _matmul(o.reshape(bp * tp, d), [wo], _ep_residual, [F32], [hp], tm=tm_p, tn=256,
                            name=f"attn_o_p{layer}")

            (xn,) = _rmsnorm(hs, [mixer_norm[layer]], [BF16], f"mixer_norm_s{layer}")
            (q,) = _matmul(xn, [wq], _ep_identity, [F32], tm=tm_s, tn=1024, name=f"attn_q_s{layer}")
            q3 = q.reshape(bs, n_heads, hd)
            sel = _block_select(q3, kmean_s, n_sel_s)
            o = _moba_step(q3, k_s.reshape(bs, n_heads, hd), v_s.reshape(bs, n_heads, hd),
                           cache_k, cache_v, page_table, sel)
            (hs,) = _matmul(o.astype(BF16), [wo], _ep_residual, [F32], [hs], tm=tm_s, tn=1024,
                            name=f"attn_o_s{layer}")

        wg, wu, wd = bf(w_gate[layer]), bf(w_up[layer]), bf(w_down[layer])
        hp = _swiglu_block(hp, ffn_norm[layer], wg, wu, wd, f"p{layer}", tm=tm_p)
        hs = _swiglu_block(hs, ffn_norm[layer], wg, wu, wd, f"s{layer}", tm=tm_s)

        if layer == n_a - 1:
            wk, wv = bf(w_k), bf(w_v)
            (xn,) = _rmsnorm(hp, [kv_norm], [BF16], "kv_norm_p")
            k_p, v_p = _matmul(xn, [wk, wv], _ep_identity, [F32, F32], tm=tm_p, tn=256, name="kv_p")
            (xn,) = _rmsnorm(hs, [kv_norm], [BF16], "kv_norm_s")
            k_s, v_s = _matmul(xn, [wk, wv], _ep_identity, [F32, F32], tm=tm_s, tn=1024, name="kv_s")
            kmean_s = _page_block_means(cache_k, page_table)

    (y_p,) = _rmsnorm(hp, [final_norm], [F32], "final_norm_p")
    (y_s,) = _rmsnorm(hs, [final_norm], [F32], "final_norm_s")
    return (y_p.reshape(bp, tp, d), y_s.reshape(bs, ts, d),
            k_p.reshape(bp, tp, n_heads, hd), v_p.reshape(bp, tp, n_heads, hd),
            k_s.reshape(bs, ts, n_heads, hd), v_s.reshape(bs, ts, n_heads, hd),
            jnp.stack(conv_p), jnp.stack(conv_s), jnp.stack(hl_p), jnp.stack(hl_s))
```

```python
import functools

import jax
import jax.numpy as jnp
from jax import lax
from jax.experimental import pallas as pl
from jax.experimental.pallas import tpu as pltpu

F32 = jnp.float32
BF16 = jnp.bfloat16

RMS_EPS = 1e-6
LRU_C = 8.0
MOBA_BLOCK = 256
MOBA_TOP_K = 3
LOG2_E = 1.4426950408889634

V7X_VMEM_BYTES = 64 * 1024 * 1024
V7X_SUBLANES = 8
V7X_LANES = 128
_VMEM_INTERNAL_BYTES = 8 * 1024 * 1024


def _pick_tile(dim, target, align):
    best = None
    d = align
    while d <= min(dim, target):
        if dim % d == 0:
            best = d
        d += align
    return best if best is not None else dim


def _nbytes(shape, dtype):
    n = 1
    for s in shape:
        n *= s
    return n * jnp.dtype(dtype).itemsize


def _params(semantics, block_bytes, scratch_bytes=0):
    limit = 2 * block_bytes + scratch_bytes + _VMEM_INTERNAL_BYTES
    limit = min(max(limit, 16 * 1024 * 1024), V7X_VMEM_BYTES - 4 * 1024 * 1024)
    return pltpu.CompilerParams(dimension_semantics=semantics, vmem_limit_bytes=int(limit))


def _rmsnorm_kernel(x_ref, *refs, n_out):
    g_refs, o_refs = refs[:n_out], refs[n_out:]
    x = x_ref[...].astype(F32)
    y = x * lax.rsqrt(jnp.mean(x * x, axis=-1, keepdims=True) + RMS_EPS)
    for g_ref, o_ref in zip(g_refs, o_refs):
        o_ref[...] = (y * g_ref[...].astype(F32)).astype(o_ref.dtype)


def _rmsnorm(x, gains, out_dtypes, name):
    m, d = x.shape
    tm = _pick_tile(m, 256, V7X_SUBLANES)
    n_out = len(gains)
    block_bytes = _nbytes((tm, d), x.dtype) + sum(_nbytes((tm, d), dt) for dt in out_dtypes)
    outs = pl.pallas_call(
        functools.partial(_rmsnorm_kernel, n_out=n_out),
        grid=(m // tm,),
        in_specs=[pl.BlockSpec((tm, d), lambda i: (i, 0))]
        + [pl.BlockSpec((1, d), lambda i: (0, 0))] * n_out,
        out_specs=[pl.BlockSpec((tm, d), lambda i: (i, 0))] * n_out,
        out_shape=[jax.ShapeDtypeStruct((m, d), dt) for dt in out_dtypes],
        compiler_params=_params(("parallel",), block_bytes),
        name=name,
    )(x, *[g.reshape(1, d) for g in gains])
    return outs


def _matmul_kernel(*refs, n_w, n_extra, n_out, epilogue):
    ap_ref, as_ref = refs[0], refs[1]
    pos = 2
    w_refs = refs[pos:pos + n_w]
    pos += n_w
    ep_refs = refs[pos:pos + n_extra]
    pos += n_extra
    es_refs = refs[pos:pos + n_extra]
    pos += n_extra
    op_refs = refs[pos:pos + n_out]
    pos += n_out
    os_refs = refs[pos:pos + n_out]
    pos += n_out
    wb_refs = refs[pos:]

    def apply(a_ref, e_refs, o_refs):
        a = a_ref[...]
        accs = [jnp.dot(a, wb[...], preferred_element_type=F32) for wb in wb_refs]
        outs = epilogue(*accs, *[e[...] for e in e_refs])
        for o_ref, val in zip(o_refs, outs):
            o_ref[...] = val.astype(o_ref.dtype)

    @pl.when(pl.program_id(1) == 0)
    def _():
        for w_ref, wb_ref in zip(w_refs, wb_refs):
            wb_ref[...] = w_ref[...].astype(BF16)
        apply(as_ref, es_refs, os_refs)

    apply(ap_ref, ep_refs, op_refs)


def _matmul(a_p, a_s, ws, epilogue, out_dtypes, extras_p=(), extras_s=(), *, tm, tn,
            k_part=(0, 1), name):
    m, kdim = a_p.shape
    s = a_s.shape[0]
    n = ws[0][0].shape[2]
    kp, nkp = k_part
    assert kdim % nkp == 0
    tk = kdim // nkp
    tm = _pick_tile(m, tm, V7X_SUBLANES)
    tn = _pick_tile(n, tn, V7X_LANES)
    n_w, n_extra, n_out = len(ws), len(extras_p), len(out_dtypes)
    assert len(extras_s) == n_extra

    def w_spec(layer):
        return pl.BlockSpec((None, tk, tn), lambda j, i: (layer, kp, j))

    p_tile = pl.BlockSpec((tm, tn), lambda j, i: (i, j))
    s_tile = pl.BlockSpec((s, tn), lambda j, i: (0, j))
    block_bytes = (_nbytes((tm + s, tk), BF16) + n_w * _nbytes((tk, tn), F32)
                   + sum(_nbytes((tm + s, tn), e.dtype) for e in extras_p)
                   + sum(_nbytes((tm + s, tn), dt) for dt in out_dtypes))
    outs = pl.pallas_call(
        functools.partial(_matmul_kernel, n_w=n_w, n_extra=n_extra, n_out=n_out, epilogue=epilogue),
        grid=(n // tn, m // tm),
        in_specs=[pl.BlockSpec((tm, tk), lambda j, i: (i, kp)),
                  pl.BlockSpec((s, tk), lambda j, i: (0, kp))]
        + [w_spec(layer) for _, layer in ws]
        + [p_tile] * n_extra + [s_tile] * n_extra,
        out_specs=[p_tile] * n_out + [s_tile] * n_out,
        out_shape=[jax.ShapeDtypeStruct((m, n), dt) for dt in out_dtypes]
        + [jax.ShapeDtypeStruct((s, n), dt) for dt in out_dtypes],
        scratch_shapes=[pltpu.VMEM((tk, tn), BF16) for _ in ws],
        compiler_params=_params(("parallel", "arbitrary"), block_bytes,
                                n_w * _nbytes((tk, tn), BF16)),
        name=name,
    )(a_p, a_s, *[w for w, _ in ws], *extras_p, *extras_s)
    return outs[:n_out], outs[n_out:]


def _ep_identity(*accs):
    return accs


def _ep_x_gelu(acc_x, acc_y):
    return acc_x, jax.nn.gelu(acc_y)


def _ep_residual(acc, res):
    return (res + acc,)


def _ep_residual_partial(acc, partial, res):
    return (res + (partial + acc),)


def _ep_swiglu(acc_gate, acc_up):
    return (jax.nn.silu(acc_gate) * acc_up,)


def _lru_gates(xc, head, wa_ref, ba_ref, wi_ref, bi_ref, lam):
    xb = xc.astype(BF16)
    r = jax.nn.sigmoid(jnp.dot(xb, wa_ref[head], preferred_element_type=F32) + ba_ref[head])
    i = jax.nn.sigmoid(jnp.dot(xb, wi_ref[head], preferred_element_type=F32) + bi_ref[head])
    log_a = LRU_C * r * jax.nn.log_sigmoid(lam)
    a = jnp.exp(log_a)
    mult = jnp.sqrt(-jnp.tanh(log_a) * (a * a + 1.0))
    return a, mult, i


def _rglru_seq_kernel(xr_ref, gate_ref, cbuf_ref, h0_ref, cw_ref, cb_ref, wa_ref, ba_ref, wi_ref,
                      bi_ref, lam_ref, hg_ref, hlast_ref, xp_scr, a_scr, u_scr, hs_scr, h_scr, *,
                      tt, nh, dh, cwid, first_pos_zero):
    t = pl.program_id(2)
    halo = cwid - 1
    base = V7X_SUBLANES - halo

    @pl.when(t == 0)
    def _():
        xp_scr[base:V7X_SUBLANES, :] = cbuf_ref[0].astype(F32)
        h_scr[...] = h0_ref[0].astype(F32)

    xp_scr[V7X_SUBLANES:V7X_SUBLANES + tt, :] = xr_ref[0]
    xc = cb_ref[...] + cw_ref[0:1, :] * xp_scr[base:base + tt, :]
    for j in range(1, cwid):
        xc = xc + cw_ref[j:j + 1, :] * xp_scr[base + j:base + j + tt, :]
    xp_scr[base:V7X_SUBLANES, :] = xp_scr[base + tt:V7X_SUBLANES + tt, :]

    for hd in range(nh):
        sl = slice(hd * dh, (hd + 1) * dh)
        xh = xc[:, sl]
        a, mult, i = _lru_gates(xh, hd, wa_ref, ba_ref, wi_ref, bi_ref, lam_ref[:, sl])
        if first_pos_zero:
            row = lax.broadcasted_iota(jnp.int32, mult.shape, 0)
            mult = jnp.where((row == 0) & (t == 0), 1.0, mult)
        a_scr[:, sl] = a
        u_scr[:, sl] = mult * i * xh

    def step(s, h):
        h = a_scr[pl.ds(s, 1), :] * h + u_scr[pl.ds(s, 1), :]
        hs_scr[pl.ds(s, 1), :] = h
        return h

    h = lax.fori_loop(0, tt, step, h_scr[...], unroll=8)
    h_scr[...] = h
    hg_ref[0] = (hs_scr[...] * gate_ref[0]).astype(hg_ref.dtype)
    hlast_ref[0] = h


def _rglru_seq(xr, gate, conv_buf, h0, conv_w, conv_b, w_a, b_a, w_i, b_i, lam, *, first_pos_zero):
    b, t, w = xr.shape
    nheads, dh = w_a.shape[0], w_a.shape[1]
    cwid = conv_w.shape[0]
    wc = _pick_tile(w, 1024, dh)
    nh = wc // dh
    tt = _pick_tile(t, 256, V7X_SUBLANES)
    blk = lambda bi_, ci, ti: (bi_, ti, ci)
    per_bc = lambda bi_, ci, ti: (bi_, 0, ci)
    chan = lambda bi_, ci, ti: (0, ci)
    head = lambda bi_, ci, ti: (ci, 0, 0)
    block_bytes = (2 * _nbytes((tt, wc), F32) + _nbytes((tt, wc), BF16)
                   + 2 * _nbytes((nh, dh, dh), BF16) + 16 * _nbytes((1, wc), F32))
    scratch_bytes = 4 * _nbytes((tt + V7X_SUBLANES, wc), F32)
    hg, hlast = pl.pallas_call(
        functools.partial(_rglru_seq_kernel, tt=tt, nh=nh, dh=dh, cwid=cwid,
                          first_pos_zero=first_pos_zero),
        grid=(b, w // wc, t // tt),
        in_specs=[
            pl.BlockSpec((1, tt, wc), blk),
            pl.BlockSpec((1, tt, wc), blk),
            pl.BlockSpec((1, cwid - 1, wc), per_bc),
            pl.BlockSpec((1, 1, wc), per_bc),
            pl.BlockSpec((cwid, wc), chan),
            pl.BlockSpec((1, wc), chan),
            pl.BlockSpec((nh, dh, dh), head),
            pl.BlockSpec((nh, 1, dh), head),
            pl.BlockSpec((nh, dh, dh), head),
            pl.BlockSpec((nh, 1, dh), head),
            pl.BlockSpec((1, wc), chan),
        ],
        out_specs=[pl.BlockSpec((1, tt, wc), blk), pl.BlockSpec((1, 1, wc), per_bc)],
        out_shape=[jax.ShapeDtypeStruct((b, t, w), BF16), jax.ShapeDtypeStruct((b, 1, w), F32)],
        scratch_shapes=[
            pltpu.VMEM((tt + V7X_SUBLANES, wc), F32),
            pltpu.VMEM((tt, wc), F32),
            pltpu.VMEM((tt, wc), F32),
            pltpu.VMEM((tt, wc), F32),
            pltpu.VMEM((1, wc), F32),
        ],
        compiler_params=_params(("parallel", "parallel", "arbitrary"), block_bytes, scratch_bytes),
        name="rglru_seq",
    )(xr, gate, conv_buf, h0.reshape(b, 1, w), conv_w, conv_b.reshape(1, w),
      w_a.astype(BF16), b_a.reshape(nheads, 1, dh), w_i.astype(BF16), b_i.reshape(nheads, 1, dh),
      lam.reshape(1, w))
    return hg, hlast.reshape(b, w)


def _rglru_step_kernel(xr_ref, gate_ref, cbuf_ref, h0_ref, cw_ref, cb_ref, wa_ref, ba_ref, wi_ref,
                       bi_ref, lam_ref, hg_ref, h_ref, nbuf_ref, *, nheads, dh, cwid,
                       first_pos_zero):
    xr = xr_ref[...]
    xc = cb_ref[...] + cw_ref[cwid - 1:cwid, :] * xr
    for j in range(cwid - 1):
        xc = xc + cw_ref[j:j + 1, :] * cbuf_ref[j]
    for j in range(cwid - 2):
        nbuf_ref[j] = cbuf_ref[j + 1]
    nbuf_ref[cwid - 2] = xr
    for hd in range(nheads):
        sl = slice(hd * dh, (hd + 1) * dh)
        xh = xc[:, sl]
        a, mult, i = _lru_gates(xh, hd, wa_ref, ba_ref, wi_ref, bi_ref, lam_ref[:, sl])
        if first_pos_zero:
            mult = jnp.ones_like(mult)
        h = a * h0_ref[:, sl] + mult * i * xh
        h_ref[:, sl] = h
        hg_ref[:, sl] = (h * gate_ref[:, sl]).astype(hg_ref.dtype)


def _rglru_step(xr, gate, conv_buf, h0, conv_w, conv_b, w_a, b_a, w_i, b_i, lam, *, first_pos_zero):
    s, w = xr.shape
    nheads, dh = w_a.shape[0], w_a.shape[1]
    cwid = conv_w.shape[0]
    hg, h, nbuf = pl.pallas_call(
        functools.partial(_rglru_step_kernel, nheads=nheads, dh=dh, cwid=cwid,
                          first_pos_zero=first_pos_zero),
        out_shape=[jax.ShapeDtypeStruct((s, w), BF16), jax.ShapeDtypeStruct((s, w), F32),
                   jax.ShapeDtypeStruct((cwid - 1, s, w), F32)],
        name="rglru_step",
    )(xr, gate, jnp.swapaxes(conv_buf, 0, 1), h0, conv_w, conv_b.reshape(1, w),
      w_a.astype(BF16), b_a.reshape(nheads, 1, dh), w_i.astype(BF16), b_i.reshape(nheads, 1, dh),
      lam.reshape(1, w))
    return hg, h, jnp.swapaxes(nbuf, 0, 1)


def _moba_seq_kernel(q_ref, k_ref, v_ref, o_ref, *, nblk, blk, n_sel, scale):
    k32 = k_ref[0]
    kb = k32.astype(BF16)
    v_t = v_ref[0].T.astype(BF16)
    nblk_pad = -(-nblk // V7X_SUBLANES) * V7X_SUBLANES
    means = [jnp.mean(k32[b * blk:(b + 1) * blk], axis=0, keepdims=True) for b in range(nblk)]
    means += [jnp.zeros_like(means[0])] * (nblk_pad - nblk)
    kmean = jnp.concatenate(means, axis=0)
    key_i = lax.broadcasted_iota(jnp.int32, (blk, blk), 0)
    qry_i = lax.broadcasted_iota(jnp.int32, (blk, blk), 1)
    causal = key_i <= qry_i
    neg_inf = jnp.float32(-jnp.inf)
    contract_last = (((1,), (1,)), ((), ()))
    exp2_scale = scale * LOG2_E

    for c in range(nblk):
        q32 = q_ref[0, c * blk:(c + 1) * blk, :]
        qb = q32.astype(BF16)
        sels = [None] * c
        if c > n_sel:
            gate = lax.dot_general(kmean, q32, contract_last, precision=lax.Precision.HIGHEST,
                                   preferred_element_type=F32)
            g = [gate[b:b + 1, :] for b in range(c)]
            for b in range(c):
                cnt = jnp.zeros((1, blk), jnp.int32)
                for b2 in range(c):
                    if b2 == b:
                        continue
                    beats = (g[b2] >= g[b]) if b2 < b else (g[b2] > g[b])
                    cnt = cnt + beats.astype(jnp.int32)
                sels[b] = cnt < n_sel
        s_blocks = []
        for b in range(c + 1):
            s = lax.dot_general(kb[b * blk:(b + 1) * blk], qb, contract_last,
                                preferred_element_type=F32)
            if b == c:
                s = jnp.where(causal, s, neg_inf)
            elif sels[b] is not None:
                s = jnp.where(sels[b], s, neg_inf)
            s_blocks.append(s)
        m = jnp.max(s_blocks[c], axis=0, keepdims=True)
        for b in range(c):
            m = jnp.maximum(m, jnp.max(s_blocks[b], axis=0, keepdims=True))
        l = jnp.zeros((1, blk), F32)
        o_t = jnp.zeros((v_t.shape[0], blk), F32)
        for b in range(c + 1):
            p = jnp.exp2((s_blocks[b] - m) * exp2_scale)
            l = l + jnp.sum(p, axis=0, keepdims=True)
            o_t = o_t + jnp.dot(v_t[:, b * blk:(b + 1) * blk], p.astype(BF16),
                                preferred_element_type=F32)
        o_ref[0, c * blk:(c + 1) * blk, :] = (o_t / l).T.astype(o_ref.dtype)


def _moba_seq(q, k, v, n_heads):
    b, t, d = q.shape
    hd = d // n_heads
    assert t % MOBA_BLOCK == 0 and hd % V7X_LANES == 0
    nblk = t // MOBA_BLOCK
    spec = pl.BlockSpec((1, t, hd), lambda bi, hi: (bi, 0, hi))
    block_bytes = 3 * _nbytes((t, hd), F32) + _nbytes((t, hd), BF16)
    return pl.pallas_call(
        functools.partial(_moba_seq_kernel, nblk=nblk, blk=MOBA_BLOCK,
                          n_sel=min(MOBA_TOP_K, nblk - 1), scale=hd ** -0.5),
        grid=(b, n_heads),
        in_specs=[spec, spec, spec],
        out_specs=spec,
        out_shape=jax.ShapeDtypeStruct((b, t, d), BF16),
        compiler_params=_params(("parallel", "parallel"), block_bytes),
        name="moba_seq",
    )(q, k, v)


def _page_mean_kernel(pt_ref, *refs, npp, blk):
    page_refs, o_ref = refs[:npp], refs[npp]
    acc = jnp.sum(page_refs[0][0].astype(F32), axis=0)
    for p_ref in page_refs[1:]:
        acc = acc + jnp.sum(p_ref[0].astype(F32), axis=0)
    o_ref[0, 0] = acc / blk


def _page_block_means(cache_k, page_table):
    _, page, n_heads, hd = cache_k.shape
    s, npages = page_table.shape
    npp = MOBA_BLOCK // page
    nb = npages // npp

    def page_spec(i):
        return pl.BlockSpec((1, page, n_heads, hd),
                            lambda si, bi, pt: (pt[si * npages + bi * npp + i], 0, 0, 0))

    grid_spec = pltpu.PrefetchScalarGridSpec(
        num_scalar_prefetch=1,
        grid=(s, nb),
        in_specs=[page_spec(i) for i in range(npp)],
        out_specs=pl.BlockSpec((1, 1, n_heads, hd), lambda si, bi, pt: (si, bi, 0, 0)),
    )
    block_bytes = npp * _nbytes((page, n_heads, hd), cache_k.dtype) + _nbytes((n_heads, hd), F32)
    return pl.pallas_call(
        functools.partial(_page_mean_kernel, npp=npp, blk=MOBA_BLOCK),
        grid_spec=grid_spec,
        out_shape=jax.ShapeDtypeStruct((s, nb, n_heads, hd), F32),
        compiler_params=_params(("parallel", "arbitrary"), block_bytes),
        name="page_block_means",
    )(page_table.reshape(-1), *([cache_k] * npp))


def _block_select_kernel(q_ref, km_ref, sel_ref, *, n_sel):
    q = q_ref[0]
    km = km_ref[0]
    nb = km.shape[0]
    g = jnp.sum(km * q[None], axis=-1)
    blk_i = lax.broadcasted_iota(jnp.int32, g.shape, 0)
    rows = []
    for _ in range(n_sel):
        m = jnp.max(g, axis=0, keepdims=True)
        idx = jnp.min(jnp.where(g == m, blk_i, nb), axis=0, keepdims=True)
        rows.append(idx)
        g = jnp.where(blk_i == idx, -jnp.inf, g)
    sel_ref[0] = jnp.concatenate(rows, axis=0)


def _block_select(q, kmean, n_sel):
    s, nb, n_heads, hd = kmean.shape
    return pl.pallas_call(
        functools.partial(_block_select_kernel, n_sel=n_sel),
        grid=(s,),
        in_specs=[pl.BlockSpec((1, n_heads, hd), lambda si: (si, 0, 0)),
                  pl.BlockSpec((1, nb, n_heads, hd), lambda si: (si, 0, 0, 0))],
        out_specs=pl.BlockSpec((1, n_sel, n_heads), lambda si: (si, 0, 0)),
        out_shape=jax.ShapeDtypeStruct((s, n_sel, n_heads), jnp.int32),
        name="block_select",
    )(q, kmean)


def _moba_step_kernel(pt_ref, sel_ref, q_ref, ks_ref, vs_ref, *refs, n_kv, hpg, scale):
    k_refs, v_refs, o_ref = refs[:n_kv], refs[n_kv:2 * n_kv], refs[2 * n_kv]
    hh = pl.program_id(1) % hpg
    q = q_ref[0]
    hd = q.shape[-1]
    rows = 2 * V7X_SUBLANES
    qb = jnp.broadcast_to(q, (rows, hd)).astype(BF16)
    page = k_refs[0].shape[1]
    lane = lax.broadcasted_iota(jnp.int32, (1, page * hpg), 1)
    mine = (lane & (hpg - 1)) == hh
    contract_last = (((1,), (1,)), ((), ()))
    scores = []
    for k_ref in k_refs:
        k2 = k_ref[0].reshape(page * hpg, hd).astype(BF16)
        s = lax.dot_general(qb, k2, contract_last, preferred_element_type=F32)[0:1] * scale
        scores.append(jnp.where(mine, s, -jnp.inf))
    s_own = jnp.sum(q * ks_ref[0], axis=-1, keepdims=True) * scale
    m = s_own
    for s in scores:
        m = jnp.maximum(m, jnp.max(s, axis=-1, keepdims=True))
    p_own = jnp.exp(s_own - m)
    l = p_own
    o = p_own * vs_ref[0]
    for s, v_ref in zip(scores, v_refs):
        p = jnp.exp(s - m)
        l = l + jnp.sum(p, axis=-1, keepdims=True)
        v2 = v_ref[0].reshape(page * hpg, hd).astype(BF16)
        pb = jnp.broadcast_to(p, (rows, page * hpg)).astype(BF16)
        o = o + jnp.dot(pb, v2, preferred_element_type=F32)[0:1]
    o_ref[0] = (o / l).astype(o_ref.dtype)


def _moba_step(q, k_new, v_new, cache_k, cache_v, page_table, sel):
    s, n_heads, hd = q.shape
    _, page, _, _ = cache_k.shape
    npages = page_table.shape[1]
    n_sel = sel.shape[1]
    npp = MOBA_BLOCK // page
    hpg = V7X_SUBLANES
    assert n_heads % hpg == 0 and hpg & (hpg - 1) == 0
    n_kv = n_sel * npp
    row = lambda si, hi, pt, sl: (si * n_heads + hi, 0, 0)

    def page_spec(r, p):
        def index(si, hi, pt, sl):
            blk = sl[(si * n_sel + r) * n_heads + hi]
            return (pt[si * npages + blk * npp + p], 0, hi // hpg, 0)
        return pl.BlockSpec((1, page, hpg, hd), index)

    kv_specs = [page_spec(r, p) for r in range(n_sel) for p in range(npp)]
    grid_spec = pltpu.PrefetchScalarGridSpec(
        num_scalar_prefetch=2,
        grid=(s, n_heads),
        in_specs=[pl.BlockSpec((1, 1, hd), row)] * 3 + kv_specs + kv_specs,
        out_specs=pl.BlockSpec((1, 1, hd), row),
    )
    block_bytes = 2 * n_kv * _nbytes((page, hpg, hd), cache_k.dtype) + 4 * _nbytes((1, hd), F32)
    flat = lambda x: x.reshape(s * n_heads, 1, hd)
    out = pl.pallas_call(
        functools.partial(_moba_step_kernel, n_kv=n_kv, hpg=hpg, scale=hd ** -0.5),
        grid_spec=grid_spec,
        out_shape=jax.ShapeDtypeStruct((s * n_heads, 1, hd), F32),
        compiler_params=_params(("parallel", "parallel"), block_bytes),
        name="moba_step",
    )(page_table.reshape(-1), sel.reshape(-1), flat(q), flat(k_new), flat(v_new),
      *([cache_k] * n_kv), *([cache_v] * n_kv))
    return out.reshape(s, n_heads * hd)


def _swiglu_block(hp, hs, norm_gain, w_gate, w_up, w_down, layer, tag):
    (xp,) = _rmsnorm(hp, [norm_gain], [BF16], f"ffn_norm_p{tag}")
    (xs,) = _rmsnorm(hs, [norm_gain], [BF16], f"ffn_norm_s{tag}")
    (ap,), (as_,) = _matmul(xp, xs, [(w_gate, layer), (w_up, layer)], _ep_swiglu, [BF16],
                            tm=1024, tn=256, name=f"ffn_gate_up{tag}")
    wd = [(w_down, layer)]
    (pp,), (ps,) = _matmul(ap, as_, wd, _ep_identity, [F32], tm=512, tn=512, k_part=(0, 2),
                           name=f"ffn_down_a{tag}")
    (hp,), (hs,) = _matmul(ap, as_, wd, _ep_residual_partial, [F32], [pp, hp], [ps, hs],
                           tm=512, tn=512, k_part=(1, 2), name=f"ffn_down_b{tag}")
    return hp, hs


def kernel(x_prompt, x_sample, cache_k, cache_v, page_table, state_conv, state_h, mixer_norm, lru_w_x, lru_w_y, lru_conv_w, lru_conv_b, lru_w_a, lru_b_a, lru_w_i, lru_b_i, lru_lambda, lru_w_out, kv_norm, w_k, w_v, w_q, w_o, ffn_norm, w_gate, w_up, w_down, final_norm):
    bp, tp, d = x_prompt.shape
    bs, ts, _ = x_sample.shape
    n_heads, hd = cache_k.shape[2], cache_k.shape[3]
    page = cache_k.shape[1]
    past_len = page_table.shape[1] * page
    depth = mixer_norm.shape[0]
    n_a = lru_w_x.shape[0]
    assert ts == 1, "the decode group advances one position per step"
    assert MOBA_BLOCK % page == 0 and past_len % MOBA_BLOCK == 0
    n_sel_s = min(MOBA_TOP_K, past_len // MOBA_BLOCK)
    assert past_len // MOBA_BLOCK >= n_sel_s

    hp = x_prompt.reshape(bp * tp, d)
    hs = x_sample.reshape(bs * ts, d)
    conv_p, conv_s, hl_p, hl_s = [], [], [], []
    k_p = v_p = k_s = v_s = kmean_s = None

    for layer in range(depth):
        (xp,) = _rmsnorm(hp, [mixer_norm[layer]], [BF16], f"mixer_norm_p{layer}")
        (xs,) = _rmsnorm(hs, [mixer_norm[layer]], [BF16], f"mixer_norm_s{layer}")
        if layer < n_a:
            a = layer
            lru = (lru_conv_w[a], lru_conv_b[a], lru_w_a[a], lru_b_a[a], lru_w_i[a], lru_b_i[a],
                   lru_lambda[a])
            cwid = lru_conv_w.shape[1]
            (xr_p, gate_p), (xr_s, gate_s) = _matmul(
                xp, xs, [(lru_w_x, a), (lru_w_y, a)], _ep_x_gelu, [F32, F32], tm=1024, tn=256,
                name=f"lru_in{layer}")
            w = xr_p.shape[1]
            xr3 = xr_p.reshape(bp, tp, w)
            hg_p, hlast_p = _rglru_seq(xr3, gate_p.reshape(bp, tp, w),
                                       jnp.zeros((bp, cwid - 1, w), F32), jnp.zeros((bp, w), F32),
                                       *lru, first_pos_zero=True)
            hg_s, hlast_s, nbuf = _rglru_step(xr_s, gate_s, state_conv[a], state_h[a], *lru,
                                              first_pos_zero=(past_len == 0))
            (hp,), (hs,) = _matmul(hg_p.reshape(bp * tp, w), hg_s, [(lru_w_out, a)], _ep_residual,
                                   [F32], [hp], [hs], tm=1024, tn=512, name=f"lru_out{layer}")
            conv_p.append(xr3[:, tp - (cwid - 1):, :])
            conv_s.append(nbuf)
            hl_p.append(hlast_p)
            hl_s.append(hlast_s)
        else:
            b = layer - n_a
            (q_p,), (q_s,) = _matmul(xp, xs, [(w_q, b)], _ep_identity, [F32], tm=1024, tn=512,
                                     name=f"attn_q{layer}")
            o_p = _moba_seq(q_p.reshape(bp, tp, d), k_p.reshape(bp, tp, d), v_p.reshape(bp, tp, d),
                            n_heads)
            q3 = q_s.reshape(bs, n_heads, hd)
            sel = _block_select(q3, kmean_s, n_sel_s)
            o_s = _moba_step(q3, k_s.reshape(bs, n_heads, hd), v_s.reshape(bs, n_heads, hd),
                             cache_k, cache_v, page_table, sel)
            (hp,), (hs,) = _matmul(o_p.reshape(bp * tp, d), o_s.astype(BF16), [(w_o, b)],
                                   _ep_residual, [F32], [hp], [hs], tm=1024, tn=512,
                                   name=f"attn_o{layer}")

        hp, hs = _swiglu_block(hp, hs, ffn_norm[layer], w_gate, w_up, w_down, layer, str(layer))

        if layer == n_a - 1:
            (xp,) = _rmsnorm(hp, [kv_norm], [BF16], "kv_norm_p")
            (xs,) = _rmsnorm(hs, [kv_norm], [BF16], "kv_norm_s")
            (k_p, v_p), (k_s, v_s) = _matmul(
                xp, xs, [(w_k.reshape(1, *w_k.shape), 0), (w_v.reshape(1, *w_v.shape), 0)],
                _ep_identity, [F32, F32], tm=1024, tn=256, name="kv")
            kmean_s = _page_block_means(cache_k, page_table)

    (y_p,) = _rmsnorm(hp, [final_norm], [F32], "final_norm_p")
    (y_s,) = _rmsnorm(hs, [final_norm], [F32], "final_norm_s")
    return (y_p.reshape(bp, tp, d), y_s.reshape(bs, ts, d),
            k_p.reshape(bp, tp, n_heads, hd), v_p.reshape(bp, tp, n_heads, hd),
            k_s.reshape(bs, ts, n_heads, hd), v_s.reshape(bs, ts, n_heads, hd),
            jnp.stack(conv_p), jnp.stack(conv_s), jnp.stack(hl_p), jnp.stack(hl_s))
```

```python
import functools

import jax
import jax.numpy as jnp
from jax import lax
from jax.experimental import pallas as pl
from jax.experimental.pallas import tpu as pltpu

F32 = jnp.float32
BF16 = jnp.bfloat16

RMS_EPS = 1e-6
LRU_C = 8.0
MOBA_BLOCK = 256
MOBA_TOP_K = 3
LOG2_E = 1.4426950408889634

V7X_VMEM_BYTES = 64 * 1024 * 1024
V7X_SUBLANES = 8
V7X_LANES = 128
_VMEM_INTERNAL_BYTES = 8 * 1024 * 1024


def _pick_tile(dim, target, align):
    best = None
    d = align
    while d <= min(dim, target):
        if dim % d == 0:
            best = d
        d += align
    return best if best is not None else dim


def _nbytes(shape, dtype):
    n = 1
    for s in shape:
        n *= s
    return n * jnp.dtype(dtype).itemsize


def _params(semantics, block_bytes, scratch_bytes=0):
    limit = 2 * block_bytes + scratch_bytes + _VMEM_INTERNAL_BYTES
    limit = min(max(limit, 16 * 1024 * 1024), V7X_VMEM_BYTES - 4 * 1024 * 1024)
    return pltpu.CompilerParams(dimension_semantics=semantics, vmem_limit_bytes=int(limit))


def _rmsnorm_kernel(x_ref, *refs, n_out):
    g_refs, o_refs = refs[:n_out], refs[n_out:]
    x = x_ref[...].astype(F32)
    y = x * lax.rsqrt(jnp.mean(x * x, axis=-1, keepdims=True) + RMS_EPS)
    for g_ref, o_ref in zip(g_refs, o_refs):
        o_ref[...] = (y * g_ref[...].astype(F32)).astype(o_ref.dtype)


def _rmsnorm(x, gains, out_dtypes, name):
    m, d = x.shape
    tm = _pick_tile(m, 256, V7X_SUBLANES)
    n_out = len(gains)
    block_bytes = _nbytes((tm, d), x.dtype) + sum(_nbytes((tm, d), dt) for dt in out_dtypes)
    outs = pl.pallas_call(
        functools.partial(_rmsnorm_kernel, n_out=n_out),
        grid=(m // tm,),
        in_specs=[pl.BlockSpec((tm, d), lambda i: (i, 0))]
        + [pl.BlockSpec((1, d), lambda i: (0, 0))] * n_out,
        out_specs=[pl.BlockSpec((tm, d), lambda i: (i, 0))] * n_out,
        out_shape=[jax.ShapeDtypeStruct((m, d), dt) for dt in out_dtypes],
        compiler_params=_params(("parallel",), block_bytes),
        name=name,
    )(x, *[g.reshape(1, d) for g in gains])
    return outs


def _matmul_kernel(*refs, n_w, n_extra, n_out, epilogue):
    ap_ref, as_ref = refs[0], refs[1]
    pos = 2
    w_refs = refs[pos:pos + n_w]
    pos += n_w
    ep_refs = refs[pos:pos + n_extra]
    pos += n_extra
    es_refs = refs[pos:pos + n_extra]
    pos += n_extra
    op_refs = refs[pos:pos + n_out]
    pos += n_out
    os_refs = refs[pos:pos + n_out]
    pos += n_out
    wb_refs = refs[pos:]

    def apply(a_ref, e_refs, o_refs):
        a = a_ref[...]
        accs = [jnp.dot(a, wb[...], preferred_element_type=F32) for wb in wb_refs]
        outs = epilogue(*accs, *[e[...] for e in e_refs])
        for o_ref, val in zip(o_refs, outs):
            o_ref[...] = val.astype(o_ref.dtype)

    @pl.when(pl.program_id(1) == 0)
    def _():
        for w_ref, wb_ref in zip(w_refs, wb_refs):
            wb_ref[...] = w_ref[...].astype(BF16)
        apply(as_ref, es_refs, os_refs)

    apply(ap_ref, ep_refs, op_refs)


def _matmul(a_p, a_s, ws, epilogue, out_dtypes, extras_p=(), extras_s=(), *, tm, tn,
            k_part=(0, 1), name):
    m, kdim = a_p.shape
    s = a_s.shape[0]
    n = ws[0][0].shape[2]
    kp, nkp = k_part
    assert kdim % nkp == 0
    tk = kdim // nkp
    tm = _pick_tile(m, tm, V7X_SUBLANES)
    tn = _pick_tile(n, tn, V7X_LANES)
    n_w, n_extra, n_out = len(ws), len(extras_p), len(out_dtypes)
    assert len(extras_s) == n_extra

    def w_spec(layer):
        return pl.BlockSpec((None, tk, tn), lambda j, i: (layer, kp, j))

    p_tile = pl.BlockSpec((tm, tn), lambda j, i: (i, j))
    s_tile = pl.BlockSpec((s, tn), lambda j, i: (0, j))
    block_bytes = (_nbytes((tm + s, tk), BF16) + n_w * _nbytes((tk, tn), F32)
                   + sum(_nbytes((tm + s, tn), e.dtype) for e in extras_p)
                   + sum(_nbytes((tm + s, tn), dt) for dt in out_dtypes))
    outs = pl.pallas_call(
        functools.partial(_matmul_kernel, n_w=n_w, n_extra=n_extra, n_out=n_out, epilogue=epilogue),
        grid=(n // tn, m // tm),
        in_specs=[pl.BlockSpec((tm, tk), lambda j, i: (i, kp)),
                  pl.BlockSpec((s, tk), lambda j, i: (0, kp))]
        + [w_spec(layer) for _, layer in ws]
        + [p_tile] * n_extra + [s_tile] * n_extra,
        out_specs=[p_tile] * n_out + [s_tile] * n_out,
        out_shape=[jax.ShapeDtypeStruct((m, n), dt) for dt in out_dtypes]
        + [jax.ShapeDtypeStruct((s, n), dt) for dt in out_dtypes],
        scratch_shapes=[pltpu.VMEM((tk, tn), BF16) for _ in ws],
        compiler_params=_params(("parallel", "arbitrary"), block_bytes,
                                n_w * _nbytes((tk, tn), BF16)),
        name=name,
    )(a_p, a_s, *[w for w, _ in ws], *extras_p, *extras_s)
    return outs[:n_out], outs[n_out:]


def _ep_identity(*accs):
    return accs


def _ep_x_gelu(acc_x, acc_y):
    return acc_x, jax.nn.gelu(acc_y)


def _ep_residual(acc, res):
    return (res + acc,)


def _ep_residual_partial(acc, partial, res):
    return (res + (partial + acc),)


def _ep_swiglu(acc_gate, acc_up):
    return (jax.nn.silu(acc_gate) * acc_up,)


def _lru_gates(xc, head, wa_ref, ba_ref, wi_ref, bi_ref, lam):
    xb = xc.astype(BF16)
    r = jax.nn.sigmoid(jnp.dot(xb, wa_ref[head], preferred_element_type=F32) + ba_ref[head])
    i = jax.nn.sigmoid(jnp.dot(xb, wi_ref[head], preferred_element_type=F32) + bi_ref[head])
    log_a = LRU_C * r * jax.nn.log_sigmoid(lam)
    a = jnp.exp(log_a)
    mult = jnp.sqrt(-jnp.tanh(log_a) * (a * a + 1.0))
    return a, mult, i


def _rglru_seq_kernel(xr_ref, gate_ref, cbuf_ref, h0_ref, cw_ref, cb_ref, wa_ref, ba_ref, wi_ref,
                      bi_ref, lam_ref, hg_ref, hlast_ref, xp_scr, a_scr, u_scr, hs_scr, h_scr, *,
                      tt, nh, dh, cwid, first_pos_zero):
    t = pl.program_id(2)
    halo = cwid - 1
    base = V7X_SUBLANES - halo

    @pl.when(t == 0)
    def _():
        xp_scr[base:V7X_SUBLANES, :] = cbuf_ref[0].astype(F32)
        h_scr[...] = h0_ref[0].astype(F32)

    xp_scr[V7X_SUBLANES:V7X_SUBLANES + tt, :] = xr_ref[0]
    xc = cb_ref[...] + cw_ref[0:1, :] * xp_scr[base:base + tt, :]
    for j in range(1, cwid):
        xc = xc + cw_ref[j:j + 1, :] * xp_scr[base + j:base + j + tt, :]
    xp_scr[base:V7X_SUBLANES, :] = xp_scr[base + tt:V7X_SUBLANES + tt, :]

    for hd in range(nh):
        sl = slice(hd * dh, (hd + 1) * dh)
        xh = xc[:, sl]
        a, mult, i = _lru_gates(xh, hd, wa_ref, ba_ref, wi_ref, bi_ref, lam_ref[:, sl])
        ix = i * xh
        a_scr[:, sl] = a
        u_scr[:, sl] = mult * ix
        if first_pos_zero:
            @pl.when(t == 0)
            def _():
                u_scr[0:1, sl] = ix[0:1, :]

    def step(s, h):
        h = a_scr[pl.ds(s, 1), :] * h + u_scr[pl.ds(s, 1), :]
        hs_scr[pl.ds(s, 1), :] = h
        return h

    h = lax.fori_loop(0, tt, step, h_scr[...], unroll=8)
    h_scr[...] = h
    hg_ref[0] = (hs_scr[...] * gate_ref[0]).astype(hg_ref.dtype)
    hlast_ref[0] = h


def _rglru_seq(xr, gate, conv_buf, h0, conv_w, conv_b, w_a, b_a, w_i, b_i, lam, *, first_pos_zero):
    b, t, w = xr.shape
    nheads, dh = w_a.shape[0], w_a.shape[1]
    cwid = conv_w.shape[0]
    wc = _pick_tile(w, 1024, dh)
    nh = wc // dh
    tt = _pick_tile(t, 512, V7X_SUBLANES)
    blk = lambda bi_, ci, ti: (bi_, ti, ci)
    per_bc = lambda bi_, ci, ti: (bi_, 0, ci)
    chan = lambda bi_, ci, ti: (0, ci)
    head = lambda bi_, ci, ti: (ci, 0, 0)
    block_bytes = (2 * _nbytes((tt, wc), F32) + _nbytes((tt, wc), BF16)
                   + 2 * _nbytes((nh, dh, dh), BF16) + 16 * _nbytes((1, wc), F32))
    scratch_bytes = 4 * _nbytes((tt + V7X_SUBLANES, wc), F32)
    hg, hlast = pl.pallas_call(
        functools.partial(_rglru_seq_kernel, tt=tt, nh=nh, dh=dh, cwid=cwid,
                          first_pos_zero=first_pos_zero),
        grid=(b, w // wc, t // tt),
        in_specs=[
            pl.BlockSpec((1, tt, wc), blk),
            pl.BlockSpec((1, tt, wc), blk),
            pl.BlockSpec((1, cwid - 1, wc), per_bc),
            pl.BlockSpec((1, 1, wc), per_bc),
            pl.BlockSpec((cwid, wc), chan),
            pl.BlockSpec((1, wc), chan),
            pl.BlockSpec((nh, dh, dh), head),
            pl.BlockSpec((nh, 1, dh), head),
            pl.BlockSpec((nh, dh, dh), head),
            pl.BlockSpec((nh, 1, dh), head),
            pl.BlockSpec((1, wc), chan),
        ],
        out_specs=[pl.BlockSpec((1, tt, wc), blk), pl.BlockSpec((1, 1, wc), per_bc)],
        out_shape=[jax.ShapeDtypeStruct((b, t, w), BF16), jax.ShapeDtypeStruct((b, 1, w), F32)],
        scratch_shapes=[
            pltpu.VMEM((tt + V7X_SUBLANES, wc), F32),
            pltpu.VMEM((tt, wc), F32),
            pltpu.VMEM((tt, wc), F32),
            pltpu.VMEM((tt, wc), F32),
            pltpu.VMEM((1, wc), F32),
        ],
        compiler_params=_params(("parallel", "parallel", "arbitrary"), block_bytes, scratch_bytes),
        name="rglru_seq",
    )(xr, gate, conv_buf, h0.reshape(b, 1, w), conv_w, conv_b.reshape(1, w),
      w_a.astype(BF16), b_a.reshape(nheads, 1, dh), w_i.astype(BF16), b_i.reshape(nheads, 1, dh),
      lam.reshape(1, w))
    return hg, hlast.reshape(b, w)


def _rglru_step_kernel(xr_ref, gate_ref, cbuf_ref, h0_ref, cw_ref, cb_ref, wa_ref, ba_ref, wi_ref,
                       bi_ref, lam_ref, hg_ref, h_ref, nbuf_ref, *, nheads, dh, cwid,
                       first_pos_zero):
    xr = xr_ref[...]
    xc = cb_ref[...] + cw_ref[cwid - 1:cwid, :] * xr
    for j in range(cwid - 1):
        xc = xc + cw_ref[j:j + 1, :] * cbuf_ref[j]
    for j in range(cwid - 2):
        nbuf_ref[j] = cbuf_ref[j + 1]
    nbuf_ref[cwid - 2] = xr
    for hd in range(nheads):
        sl = slice(hd * dh, (hd + 1) * dh)
        xh = xc[:, sl]
        a, mult, i = _lru_gates(xh, hd, wa_ref, ba_ref, wi_ref, bi_ref, lam_ref[:, sl])
        if first_pos_zero:
            mult = jnp.ones_like(mult)
        h = a * h0_ref[:, sl] + mult * i * xh
        h_ref[:, sl] = h
        hg_ref[:, sl] = (h * gate_ref[:, sl]).astype(hg_ref.dtype)


def _rglru_step(xr, gate, conv_buf, h0, conv_w, conv_b, w_a, b_a, w_i, b_i, lam, *, first_pos_zero):
    s, w = xr.shape
    nheads, dh = w_a.shape[0], w_a.shape[1]
    cwid = conv_w.shape[0]
    hg, h, nbuf = pl.pallas_call(
        functools.partial(_rglru_step_kernel, nheads=nheads, dh=dh, cwid=cwid,
                          first_pos_zero=first_pos_zero),
        out_shape=[jax.ShapeDtypeStruct((s, w), BF16), jax.ShapeDtypeStruct((s, w), F32),
                   jax.ShapeDtypeStruct((cwid - 1, s, w), F32)],
        name="rglru_step",
    )(xr, gate, jnp.swapaxes(conv_buf, 0, 1), h0, conv_w, conv_b.reshape(1, w),
      w_a.astype(BF16), b_a.reshape(nheads, 1, dh), w_i.astype(BF16), b_i.reshape(nheads, 1, dh),
      lam.reshape(1, w))
    return hg, h, jnp.swapaxes(nbuf, 0, 1)


def _moba_seq_kernel(pt_ref, q_ref, k_ref, v_ref, *refs, nblk, blk, n_sel, scale, items, npp):
    page_refs, o_ref, km_ref = refs[:items * npp], refs[items * npp], refs[items * npp + 1]
    for it in range(items):
        acc = jnp.sum(page_refs[it * npp][0].astype(F32), axis=0)
        for p in range(1, npp):
            acc = acc + jnp.sum(page_refs[it * npp + p][0].astype(F32), axis=0)
        km_ref[it] = acc / blk

    k32 = k_ref[0]
    kb = k32.astype(BF16)
    v_t = v_ref[0].T.astype(BF16)
    nblk_pad = -(-nblk // V7X_SUBLANES) * V7X_SUBLANES
    means = [jnp.mean(k32[b * blk:(b + 1) * blk], axis=0, keepdims=True) for b in range(nblk)]
    means += [jnp.zeros_like(means[0])] * (nblk_pad - nblk)
    kmean = jnp.concatenate(means, axis=0)
    key_i = lax.broadcasted_iota(jnp.int32, (blk, blk), 0)
    qry_i = lax.broadcasted_iota(jnp.int32, (blk, blk), 1)
    causal = key_i <= qry_i
    neg_inf = jnp.float32(-jnp.inf)
    contract_last = (((1,), (1,)), ((), ()))
    exp2_scale = scale * LOG2_E

    for c in range(nblk):
        q32 = q_ref[0, c * blk:(c + 1) * blk, :]
        qb = q32.astype(BF16)
        sels = [None] * c
        if c > n_sel:
            gate = lax.dot_general(kmean, q32, contract_last, precision=lax.Precision.HIGHEST,
                                   preferred_element_type=F32)
            g = [gate[b:b + 1, :] for b in range(c)]
            for b in range(c):
                cnt = jnp.zeros((1, blk), jnp.int32)
                for b2 in range(c):
                    if b2 == b:
                        continue
                    beats = (g[b2] >= g[b]) if b2 < b else (g[b2] > g[b])
                    cnt = cnt + beats.astype(jnp.int32)
                sels[b] = cnt < n_sel
        s_blocks = []
        for b in range(c + 1):
            s = lax.dot_general(kb[b * blk:(b + 1) * blk], qb, contract_last,
                                preferred_element_type=F32)
            if b == c:
                s = jnp.where(causal, s, neg_inf)
            elif sels[b] is not None:
                s = jnp.where(sels[b], s, neg_inf)
            s_blocks.append(s)
        m = jnp.max(s_blocks[c], axis=0, keepdims=True)
        for b in range(c):
            m = jnp.maximum(m, jnp.max(s_blocks[b], axis=0, keepdims=True))
        l = jnp.zeros((1, blk), F32)
        o_t = jnp.zeros((v_t.shape[0], blk), F32)
        for b in range(c + 1):
            p = jnp.exp2((s_blocks[b] - m) * exp2_scale)
            l = l + jnp.sum(p, axis=0, keepdims=True)
            o_t = o_t + jnp.dot(v_t[:, b * blk:(b + 1) * blk], p.astype(BF16),
                                preferred_element_type=F32)
        o_ref[0, c * blk:(c + 1) * blk, :] = (o_t / l).T.astype(o_ref.dtype)


def _moba_seq(q, k, v, n_heads, cache_k, page_table):
    b, t, d = q.shape
    hd = d // n_heads
    assert t % MOBA_BLOCK == 0 and hd % V7X_LANES == 0
    nblk = t // MOBA_BLOCK
    _, page, c_heads, c_hd = cache_k.shape
    s, npages = page_table.shape
    npp = MOBA_BLOCK // page
    n_items = s * (npages // npp)
    steps = b * n_heads
    items = -(-n_items // steps)

    def page_spec(it, p):
        def index(bi, hi, pt):
            item = jnp.minimum((bi * n_heads + hi) * items + it, n_items - 1)
            return (pt[item * npp + p], 0, 0, 0)
        return pl.BlockSpec((1, page, c_heads, c_hd), index)

    spec = pl.BlockSpec((1, t, hd), lambda bi, hi, pt: (bi, 0, hi))
    grid_spec = pltpu.PrefetchScalarGridSpec(
        num_scalar_prefetch=1,
        grid=(b, n_heads),
        in_specs=[spec, spec, spec] + [page_spec(it, p) for it in range(items) for p in range(npp)],
        out_specs=[spec, pl.BlockSpec((items, c_heads, c_hd),
                                      lambda bi, hi, pt: (bi * n_heads + hi, 0, 0))],
    )
    block_bytes = (3 * _nbytes((t, hd), F32) + _nbytes((t, hd), BF16)
                   + items * npp * _nbytes((page, c_heads, c_hd), cache_k.dtype)
                   + items * _nbytes((c_heads, c_hd), F32))
    o, kmean = pl.pallas_call(
        functools.partial(_moba_seq_kernel, nblk=nblk, blk=MOBA_BLOCK,
                          n_sel=min(MOBA_TOP_K, nblk - 1), scale=hd ** -0.5, items=items, npp=npp),
        grid_spec=grid_spec,
        out_shape=[jax.ShapeDtypeStruct((b, t, d), BF16),
                   jax.ShapeDtypeStruct((steps * items, c_heads, c_hd), F32)],
        compiler_params=_params(("parallel", "parallel"), block_bytes),
        name="moba_seq",
    )(page_table.reshape(-1), q, k, v, *([cache_k] * (items * npp)))
    return o, kmean[:n_items].reshape(s, npages // npp, c_heads, c_hd)


def _block_select_kernel(q_ref, km_ref, sel_ref, *, n_sel):
    q = q_ref[0]
    km = km_ref[0]
    nb = km.shape[0]
    g = jnp.sum(km * q[None], axis=-1)
    blk_i = lax.broadcasted_iota(jnp.int32, g.shape, 0)
    rows = []
    for _ in range(n_sel):
        m = jnp.max(g, axis=0, keepdims=True)
        idx = jnp.min(jnp.where(g == m, blk_i, nb), axis=0, keepdims=True)
        rows.append(idx)
        g = jnp.where(blk_i == idx, -jnp.inf, g)
    sel_ref[0] = jnp.concatenate(rows, axis=0)


def _block_select(q, kmean, n_sel):
    s, nb, n_heads, hd = kmean.shape
    return pl.pallas_call(
        functools.partial(_block_select_kernel, n_sel=n_sel),
        grid=(s,),
        in_specs=[pl.BlockSpec((1, n_heads, hd), lambda si: (si, 0, 0)),
                  pl.BlockSpec((1, nb, n_heads, hd), lambda si: (si, 0, 0, 0))],
        out_specs=pl.BlockSpec((1, n_sel, n_heads), lambda si: (si, 0, 0)),
        out_shape=jax.ShapeDtypeStruct((s, n_sel, n_heads), jnp.int32),
        name="block_select",
    )(q, kmean)


def _moba_step_kernel(pt_ref, sel_ref, q_ref, kn_ref, vn_ref, ck_hbm, cv_hbm, o_ref, kbuf, vbuf,
                      sems, *, n_sel, npp, npages, scale):
    si, ns = pl.program_id(0), pl.num_programs(0)
    n_heads = q_ref.shape[1]
    page = ck_hbm.shape[1]
    group = n_heads // 2

    def gather(seq, grp):
        copies = []
        for hl in range(group):
            h = grp * group + hl
            for r in range(n_sel):
                blk = sel_ref[(seq * n_sel + r) * n_heads + h]
                for p in range(npp):
                    pg = pt_ref[seq * npages + blk * npp + p]
                    rows = pl.ds((r * npp + p) * page, page)
                    copies.append(pltpu.make_async_copy(
                        ck_hbm.at[pg, :, h, :], kbuf.at[grp, hl, rows, :], sems.at[0, grp]))
                    copies.append(pltpu.make_async_copy(
                        cv_hbm.at[pg, :, h, :], vbuf.at[grp, hl, rows, :], sems.at[1, grp]))
        return copies

    def attend(grp):
        for hl in range(group):
            h = grp * group + hl
            q = q_ref[0, h:h + 1, :]
            s = jnp.sum(kbuf[grp, hl] * q, axis=-1, keepdims=True) * scale
            s_own = jnp.sum(kn_ref[0, h:h + 1, :] * q, axis=-1, keepdims=True) * scale
            m = jnp.maximum(jnp.max(s, axis=0, keepdims=True), s_own)
            p = jnp.exp(s - m)
            p_own = jnp.exp(s_own - m)
            l = jnp.sum(p, axis=0, keepdims=True) + p_own
            o = jnp.sum(p * vbuf[grp, hl], axis=0, keepdims=True) + p_own * vn_ref[0, h:h + 1, :]
            o_ref[0, h:h + 1, :] = (o / l).astype(o_ref.dtype)

    @pl.when(si == 0)
    def _():
        for grp in range(2):
            for cp in gather(si, grp):
                cp.start()

    for grp in range(2):
        for cp in gather(si, grp):
            cp.wait()
        attend(grp)

        @pl.when(si + 1 < ns)
        def _():
            for cp in gather(si + 1, grp):
                cp.start()


def _moba_step(q, k_new, v_new, cache_k, cache_v, page_table, sel):
    s, n_heads, hd = q.shape
    _, page, _, _ = cache_k.shape
    npages = page_table.shape[1]
    n_sel = sel.shape[1]
    npp = MOBA_BLOCK // page
    assert n_heads % 2 == 0
    keys = n_sel * MOBA_BLOCK
    row = pl.BlockSpec((1, n_heads, hd), lambda si, pt, sl: (si, 0, 0))
    grid_spec = pltpu.PrefetchScalarGridSpec(
        num_scalar_prefetch=2,
        grid=(s,),
        in_specs=[row, row, row, pl.BlockSpec(memory_space=pl.ANY),
                  pl.BlockSpec(memory_space=pl.ANY)],
        out_specs=row,
        scratch_shapes=[pltpu.VMEM((2, n_heads // 2, keys, hd), cache_k.dtype),
                        pltpu.VMEM((2, n_heads // 2, keys, hd), cache_v.dtype),
                        pltpu.SemaphoreType.DMA((2, 2))],
    )
    scratch_bytes = 2 * _nbytes((n_heads, keys, hd), cache_k.dtype)
    out = pl.pallas_call(
        functools.partial(_moba_step_kernel, n_sel=n_sel, npp=npp, npages=npages, scale=hd ** -0.5),
        grid_spec=grid_spec,
        out_shape=jax.ShapeDtypeStruct((s, n_heads, hd), F32),
        compiler_params=_params(("arbitrary",), 4 * _nbytes((n_heads, hd), F32), scratch_bytes),
        name="moba_step",
    )(page_table.reshape(-1), sel.reshape(-1), q, k_new, v_new, cache_k, cache_v)
    return out.reshape(s, n_heads * hd)


def _swiglu_block(hp, hs, norm_gain, w_gate, w_up, w_down, layer, tag):
    (xp,) = _rmsnorm(hp, [norm_gain], [BF16], f"ffn_norm_p{tag}")
    (xs,) = _rmsnorm(hs, [norm_gain], [BF16], f"ffn_norm_s{tag}")
    (ap,), (as_,) = _matmul(xp, xs, [(w_gate, layer), (w_up, layer)], _ep_swiglu, [BF16],
                            tm=1024, tn=256, name=f"ffn_gate_up{tag}")
    wd = [(w_down, layer)]
    (pp,), (ps,) = _matmul(ap, as_, wd, _ep_identity, [F32], tm=512, tn=512, k_part=(0, 2),
                           name=f"ffn_down_a{tag}")
    (hp,), (hs,) = _matmul(ap, as_, wd, _ep_residual_partial, [F32], [pp, hp], [ps, hs],
                           tm=512, tn=512, k_part=(1, 2), name=f"ffn_down_b{tag}")
    return hp, hs


def kernel(x_prompt, x_sample, cache_k, cache_v, page_table, state_conv, state_h, mixer_norm, lru_w_x, lru_w_y, lru_conv_w, lru_conv_b, lru_w_a, lru_b_a, lru_w_i, lru_b_i, lru_lambda, lru_w_out, kv_norm, w_k, w_v, w_q, w_o, ffn_norm, w_gate, w_up, w_down, final_norm):
    bp, tp, d = x_prompt.shape
    bs, ts, _ = x_sample.shape
    n_heads, hd = cache_k.shape[2], cache_k.shape[3]
    page = cache_k.shape[1]
    past_len = page_table.shape[1] * page
    depth = mixer_norm.shape[0]
    n_a = lru_w_x.shape[0]
    assert ts == 1, "the decode group advances one position per step"
    assert MOBA_BLOCK % page == 0 and past_len % MOBA_BLOCK == 0
    n_sel_s = min(MOBA_TOP_K, past_len // MOBA_BLOCK)
    assert past_len // MOBA_BLOCK >= n_sel_s

    hp = x_prompt.reshape(bp * tp, d)
    hs = x_sample.reshape(bs * ts, d)
    conv_p, conv_s, hl_p, hl_s = [], [], [], []
    k_p = v_p = k_s = v_s = None

    for layer in range(depth):
        (xp,) = _rmsnorm(hp, [mixer_norm[layer]], [BF16], f"mixer_norm_p{layer}")
        (xs,) = _rmsnorm(hs, [mixer_norm[layer]], [BF16], f"mixer_norm_s{layer}")
        if layer < n_a:
            a = layer
            lru = (lru_conv_w[a], lru_conv_b[a], lru_w_a[a], lru_b_a[a], lru_w_i[a], lru_b_i[a],
                   lru_lambda[a])
            cwid = lru_conv_w.shape[1]
            (xr_p, gate_p), (xr_s, gate_s) = _matmul(
                xp, xs, [(lru_w_x, a), (lru_w_y, a)], _ep_x_gelu, [F32, F32], tm=1024, tn=256,
                name=f"lru_in{layer}")
            w = xr_p.shape[1]
            xr3 = xr_p.reshape(bp, tp, w)
            hg_p, hlast_p = _rglru_seq(xr3, gate_p.reshape(bp, tp, w),
                                       jnp.zeros((bp, cwid - 1, w), F32), jnp.zeros((bp, w), F32),
                                       *lru, first_pos_zero=True)
            hg_s, hlast_s, nbuf = _rglru_step(xr_s, gate_s, state_conv[a], state_h[a], *lru,
                                              first_pos_zero=(past_len == 0))
            (hp,), (hs,) = _matmul(hg_p.reshape(bp * tp, w), hg_s, [(lru_w_out, a)], _ep_residual,
                                   [F32], [hp], [hs], tm=1024, tn=512, name=f"lru_out{layer}")
            conv_p.append(xr3[:, tp - (cwid - 1):, :])
            conv_s.append(nbuf)
            hl_p.append(hlast_p)
            hl_s.append(hlast_s)
        else:
            b = layer - n_a
            (q_p,), (q_s,) = _matmul(xp, xs, [(w_q, b)], _ep_identity, [F32], tm=1024, tn=512,
                                     name=f"attn_q{layer}")
            o_p, kmean_s = _moba_seq(q_p.reshape(bp, tp, d), k_p.reshape(bp, tp, d),
                                     v_p.reshape(bp, tp, d), n_heads, cache_k, page_table)
            q3 = q_s.reshape(bs, n_heads, hd)
            sel = _block_select(q3, kmean_s, n_sel_s)
            o_s = _moba_step(q3, k_s.reshape(bs, n_heads, hd), v_s.reshape(bs, n_heads, hd),
                             cache_k, cache_v, page_table, sel)
            (hp,), (hs,) = _matmul(o_p.reshape(bp * tp, d), o_s.astype(BF16), [(w_o, b)],
                                   _ep_residual, [F32], [hp], [hs], tm=1024, tn=512,
                                   name=f"attn_o{layer}")

        hp, hs = _swiglu_block(hp, hs, ffn_norm[layer], w_gate, w_up, w_down, layer, str(layer))

        if layer == n_a - 1:
            (xp,) = _rmsnorm(hp, [kv_norm], [BF16], "kv_norm_p")
            (xs,) = _rmsnorm(hs, [kv_norm], [BF16], "kv_norm_s")
            (k_p, v_p), (k_s, v_s) = _matmul(
                xp, xs, [(w_k.reshape(1, *w_k.shape), 0), (w_v.reshape(1, *w_v.shape), 0)],
                _ep_identity, [F32, F32], tm=1024, tn=256, name="kv")

    (y_p,) = _rmsnorm(hp, [final_norm], [F32], "final_norm_p")
    (y_s,) = _rmsnorm(hs, [final_norm], [F32], "final_norm_s")
    return (y_p.reshape(bp, tp, d), y_s.reshape(bs, ts, d),
            k_p.reshape(bp, tp, n_heads, hd), v_p.reshape(bp, tp, n_heads, hd),
            k_s.reshape(bs, ts, n_heads, hd), v_s.reshape(bs, ts, n_heads, hd),
            jnp.stack(conv_p), jnp.stack(conv_s), jnp.stack(hl_p), jnp.stack(hl_s))
```

```python
import functools

import jax
import jax.numpy as jnp
from jax import lax
from jax.experimental import pallas as pl
from jax.experimental.pallas import tpu as pltpu

F32 = jnp.float32
BF16 = jnp.bfloat16

RMS_EPS = 1e-6
LRU_C = 8.0
MOBA_BLOCK = 256
MOBA_TOP_K = 3
LOG2_E = 1.4426950408889634

V7X_VMEM_BYTES = 64 * 1024 * 1024
V7X_SUBLANES = 8
V7X_LANES = 128
_VMEM_INTERNAL_BYTES = 8 * 1024 * 1024


def _pick_tile(dim, target, align):
    best = None
    d = align
    while d <= min(dim, target):
        if dim % d == 0:
            best = d
        d += align
    return best if best is not None else dim


def _nbytes(shape, dtype):
    n = 1
    for s in shape:
        n *= s
    return n * jnp.dtype(dtype).itemsize


def _params(semantics, block_bytes, scratch_bytes=0):
    limit = 2 * block_bytes + scratch_bytes + _VMEM_INTERNAL_BYTES
    limit = min(max(limit, 16 * 1024 * 1024), V7X_VMEM_BYTES - 4 * 1024 * 1024)
    return pltpu.CompilerParams(dimension_semantics=semantics, vmem_limit_bytes=int(limit))


def _rmsnorm_kernel(x_ref, *refs, n_out):
    g_refs, o_refs = refs[:n_out], refs[n_out:]
    x = x_ref[...].astype(F32)
    y = x * lax.rsqrt(jnp.mean(x * x, axis=-1, keepdims=True) + RMS_EPS)
    for g_ref, o_ref in zip(g_refs, o_refs):
        o_ref[...] = (y * g_ref[...].astype(F32)).astype(o_ref.dtype)


def _rmsnorm(x, gains, out_dtypes, name):
    m, d = x.shape
    tm = _pick_tile(m, 256, V7X_SUBLANES)
    n_out = len(gains)
    block_bytes = _nbytes((tm, d), x.dtype) + sum(_nbytes((tm, d), dt) for dt in out_dtypes)
    outs = pl.pallas_call(
        functools.partial(_rmsnorm_kernel, n_out=n_out),
        grid=(m // tm,),
        in_specs=[pl.BlockSpec((tm, d), lambda i: (i, 0))]
        + [pl.BlockSpec((1, d), lambda i: (0, 0))] * n_out,
        out_specs=[pl.BlockSpec((tm, d), lambda i: (i, 0))] * n_out,
        out_shape=[jax.ShapeDtypeStruct((m, d), dt) for dt in out_dtypes],
        compiler_params=_params(("parallel",), block_bytes),
        name=name,
    )(x, *[g.reshape(1, d) for g in gains])
    return outs


def _matmul_kernel(*refs, n_w, n_extra, n_out, epilogue, s_rows, s_pad):
    ap_ref, af_ref = refs[0], refs[1]
    pos = 2
    w_refs = refs[pos:pos + n_w]
    pos += n_w
    ep_refs = refs[pos:pos + n_extra]
    pos += n_extra
    es_refs = refs[pos:pos + n_extra]
    pos += n_extra
    op_refs = refs[pos:pos + n_out]
    pos += n_out
    os_refs = refs[pos:pos + n_out]
    pos += n_out
    wb_refs = refs[pos:]

    def finish(accs, e_refs, o_refs):
        outs = epilogue(*accs, *[e[...] for e in e_refs])
        for o_ref, val in zip(o_refs, outs):
            o_ref[...] = val.astype(o_ref.dtype)

    i = pl.program_id(1)

    @pl.when(i == 0)
    def _():
        for w_ref, wb_ref in zip(w_refs, wb_refs):
            wb_ref[...] = w_ref[...].astype(BF16)
        a = af_ref[...]
        accs = [jnp.dot(a, wb[...], preferred_element_type=F32) for wb in wb_refs]
        finish([acc[:s_rows] for acc in accs], es_refs, os_refs)
        finish([acc[s_pad:] for acc in accs], ep_refs, op_refs)

    @pl.when(i > 0)
    def _():
        a = ap_ref[...]
        finish([jnp.dot(a, wb[...], preferred_element_type=F32) for wb in wb_refs], ep_refs, op_refs)


def _matmul(a_p, a_s, ws, epilogue, out_dtypes, extras_p=(), extras_s=(), *, tm, tn,
            k_part=(0, 1), name):
    m, kdim = a_p.shape
    s = a_s.shape[0]
    n = ws[0][0].shape[2]
    kp, nkp = k_part
    assert kdim % nkp == 0
    tk = kdim // nkp
    tm = _pick_tile(m, tm, V7X_SUBLANES)
    tn = _pick_tile(n, tn, V7X_LANES)
    n_w, n_extra, n_out = len(ws), len(extras_p), len(out_dtypes)
    assert len(extras_s) == n_extra
    bf16_rows = 2 * V7X_SUBLANES
    s_pad = -(-s // bf16_rows) * bf16_rows
    a_first = jnp.concatenate([a_s, jnp.zeros((s_pad - s, kdim), a_s.dtype), a_p[:tm]], axis=0)

    def w_spec(layer):
        return pl.BlockSpec((None, tk, tn), lambda j, i: (layer, kp, j))

    p_tile = pl.BlockSpec((tm, tn), lambda j, i: (i, j))
    s_tile = pl.BlockSpec((s, tn), lambda j, i: (0, j))
    first_spec = pl.BlockSpec((s_pad + tm, tk), lambda j, i: (0, kp), pipeline_mode=pl.Buffered(1))
    block_bytes = (_nbytes((tm, tk), BF16) + n_w * _nbytes((tk, tn), F32)
                   + sum(_nbytes((tm + s, tn), e.dtype) for e in extras_p)
                   + sum(_nbytes((tm + s, tn), dt) for dt in out_dtypes))
    scratch_bytes = n_w * _nbytes((tk, tn), BF16) + _nbytes((s_pad + tm, tk), BF16)
    outs = pl.pallas_call(
        functools.partial(_matmul_kernel, n_w=n_w, n_extra=n_extra, n_out=n_out, epilogue=epilogue,
                          s_rows=s, s_pad=s_pad),
        grid=(n // tn, m // tm),
        in_specs=[pl.BlockSpec((tm, tk), lambda j, i: (i, kp)), first_spec]
        + [w_spec(layer) for _, layer in ws]
        + [p_tile] * n_extra + [s_tile] * n_extra,
        out_specs=[p_tile] * n_out + [s_tile] * n_out,
        out_shape=[jax.ShapeDtypeStruct((m, n), dt) for dt in out_dtypes]
        + [jax.ShapeDtypeStruct((s, n), dt) for dt in out_dtypes],
        scratch_shapes=[pltpu.VMEM((tk, tn), BF16) for _ in ws],
        compiler_params=_params(("parallel", "arbitrary"), block_bytes, scratch_bytes),
        name=name,
    )(a_p, a_first, *[w for w, _ in ws], *extras_p, *extras_s)
    return outs[:n_out], outs[n_out:]


def _ep_identity(*accs):
    return accs


def _ep_x_gelu(acc_x, acc_y):
    return acc_x, jax.nn.gelu(acc_y)


def _ep_residual(acc, res):
    return (res + acc,)


def _ep_residual_partial(acc, partial, res):
    return (res + (partial + acc),)


def _ep_swiglu(acc_gate, acc_up):
    return (jax.nn.silu(acc_gate) * acc_up,)


def _lru_gates(xc, head, wa_ref, ba_ref, wi_ref, bi_ref, lam):
    xb = xc.astype(BF16)
    r = jax.nn.sigmoid(jnp.dot(xb, wa_ref[head], preferred_element_type=F32) + ba_ref[head])
    i = jax.nn.sigmoid(jnp.dot(xb, wi_ref[head], preferred_element_type=F32) + bi_ref[head])
    log_a = LRU_C * r * jax.nn.log_sigmoid(lam)
    a = jnp.exp(log_a)
    mult = jnp.sqrt(-jnp.tanh(log_a) * (a * a + 1.0))
    return a, mult, i


def _rglru_seq_kernel(xr_ref, gate_ref, cbuf_ref, h0_ref, cw_ref, cb_ref, wa_ref, ba_ref, wi_ref,
                      bi_ref, lam_ref, hg_ref, hlast_ref, xp_scr, a_scr, u_scr, hs_scr, h_scr, *,
                      tt, nh, dh, cwid, first_pos_zero):
    t = pl.program_id(2)
    halo = cwid - 1
    base = V7X_SUBLANES - halo

    @pl.when(t == 0)
    def _():
        xp_scr[base:V7X_SUBLANES, :] = cbuf_ref[0].astype(F32)
        h_scr[...] = h0_ref[0].astype(F32)

    xp_scr[V7X_SUBLANES:V7X_SUBLANES + tt, :] = xr_ref[0]
    xc = cb_ref[...] + cw_ref[0:1, :] * xp_scr[base:base + tt, :]
    for j in range(1, cwid):
        xc = xc + cw_ref[j:j + 1, :] * xp_scr[base + j:base + j + tt, :]
    xp_scr[base:V7X_SUBLANES, :] = xp_scr[base + tt:V7X_SUBLANES + tt, :]

    for hd in range(nh):
        sl = slice(hd * dh, (hd + 1) * dh)
        xh = xc[:, sl]
        a, mult, i = _lru_gates(xh, hd, wa_ref, ba_ref, wi_ref, bi_ref, lam_ref[:, sl])
        ix = i * xh
        a_scr[:, sl] = a
        u_scr[:, sl] = mult * ix
        if first_pos_zero:
            @pl.when(t == 0)
            def _():
                u_scr[0:1, sl] = ix[0:1, :]

    def step(s, h):
        h = a_scr[pl.ds(s, 1), :] * h + u_scr[pl.ds(s, 1), :]
        hs_scr[pl.ds(s, 1), :] = h
        return h

    h = lax.fori_loop(0, tt, step, h_scr[...], unroll=8)
    h_scr[...] = h
    hg_ref[0] = (hs_scr[...] * gate_ref[0]).astype(hg_ref.dtype)
    hlast_ref[0] = h


def _rglru_seq(xr, gate, conv_buf, h0, conv_w, conv_b, w_a, b_a, w_i, b_i, lam, *, first_pos_zero):
    b, t, w = xr.shape
    nheads, dh = w_a.shape[0], w_a.shape[1]
    cwid = conv_w.shape[0]
    wc = _pick_tile(w, 1024, dh)
    nh = wc // dh
    tt = _pick_tile(t, 256, V7X_SUBLANES)
    blk = lambda bi_, ci, ti: (bi_, ti, ci)
    per_bc = lambda bi_, ci, ti: (bi_, 0, ci)
    chan = lambda bi_, ci, ti: (0, ci)
    head = lambda bi_, ci, ti: (ci, 0, 0)
    block_bytes = (2 * _nbytes((tt, wc), F32) + _nbytes((tt, wc), BF16)
                   + 2 * _nbytes((nh, dh, dh), BF16) + 16 * _nbytes((1, wc), F32))
    scratch_bytes = 4 * _nbytes((tt + V7X_SUBLANES, wc), F32)
    hg, hlast = pl.pallas_call(
        functools.partial(_rglru_seq_kernel, tt=tt, nh=nh, dh=dh, cwid=cwid,
                          first_pos_zero=first_pos_zero),
        grid=(b, w // wc, t // tt),
        in_specs=[
            pl.BlockSpec((1, tt, wc), blk),
            pl.BlockSpec((1, tt, wc), blk),
            pl.BlockSpec((1, cwid - 1, wc), per_bc),
            pl.BlockSpec((1, 1, wc), per_bc),
            pl.BlockSpec((cwid, wc), chan),
            pl.BlockSpec((1, wc), chan),
            pl.BlockSpec((nh, dh, dh), head),
            pl.BlockSpec((nh, 1, dh), head),
            pl.BlockSpec((nh, dh, dh), head),
            pl.BlockSpec((nh, 1, dh), head),
            pl.BlockSpec((1, wc), chan),
        ],
        out_specs=[pl.BlockSpec((1, tt, wc), blk), pl.BlockSpec((1, 1, wc), per_bc)],
        out_shape=[jax.ShapeDtypeStruct((b, t, w), BF16), jax.ShapeDtypeStruct((b, 1, w), F32)],
        scratch_shapes=[
            pltpu.VMEM((tt + V7X_SUBLANES, wc), F32),
            pltpu.VMEM((tt, wc), F32),
            pltpu.VMEM((tt, wc), F32),
            pltpu.VMEM((tt, wc), F32),
            pltpu.VMEM((1, wc), F32),
        ],
        compiler_params=_params(("parallel", "parallel", "arbitrary"), block_bytes, scratch_bytes),
        name="rglru_seq",
    )(xr, gate, conv_buf, h0.reshape(b, 1, w), conv_w, conv_b.reshape(1, w),
      w_a.astype(BF16), b_a.reshape(nheads, 1, dh), w_i.astype(BF16), b_i.reshape(nheads, 1, dh),
      lam.reshape(1, w))
    return hg, hlast.reshape(b, w)


def _rglru_step_kernel(xr_ref, gate_ref, cbuf_ref, h0_ref, cw_ref, cb_ref, wa_ref, ba_ref, wi_ref,
                       bi_ref, lam_ref, hg_ref, h_ref, nbuf_ref, *, nheads, dh, cwid,
                       first_pos_zero):
    xr = xr_ref[...]
    xc = cb_ref[...] + cw_ref[cwid - 1:cwid, :] * xr
    for j in range(cwid - 1):
        xc = xc + cw_ref[j:j + 1, :] * cbuf_ref[j]
    for j in range(cwid - 2):
        nbuf_ref[j] = cbuf_ref[j + 1]
    nbuf_ref[cwid - 2] = xr
    for hd in range(nheads):
        sl = slice(hd * dh, (hd + 1) * dh)
        xh = xc[:, sl]
        a, mult, i = _lru_gates(xh, hd, wa_ref, ba_ref, wi_ref, bi_ref, lam_ref[:, sl])
        if first_pos_zero:
            mult = jnp.ones_like(mult)
        h = a * h0_ref[:, sl] + mult * i * xh
        h_ref[:, sl] = h
        hg_ref[:, sl] = (h * gate_ref[:, sl]).astype(hg_ref.dtype)


def _rglru_step(xr, gate, conv_buf, h0, conv_w, conv_b, w_a, b_a, w_i, b_i, lam, *, first_pos_zero):
    s, w = xr.shape
    nheads, dh = w_a.shape[0], w_a.shape[1]
    cwid = conv_w.shape[0]
    hg, h, nbuf = pl.pallas_call(
        functools.partial(_rglru_step_kernel, nheads=nheads, dh=dh, cwid=cwid,
                          first_pos_zero=first_pos_zero),
        out_shape=[jax.ShapeDtypeStruct((s, w), BF16), jax.ShapeDtypeStruct((s, w), F32),
                   jax.ShapeDtypeStruct((cwid - 1, s, w), F32)],
        name="rglru_step",
    )(xr, gate, jnp.swapaxes(conv_buf, 0, 1), h0, conv_w, conv_b.reshape(1, w),
      w_a.astype(BF16), b_a.reshape(nheads, 1, dh), w_i.astype(BF16), b_i.reshape(nheads, 1, dh),
      lam.reshape(1, w))
    return hg, h, jnp.swapaxes(nbuf, 0, 1)


def _moba_seq_kernel(pt_ref, q_ref, k_ref, v_ref, *refs, nblk, blk, n_sel, scale, items, npp):
    page_refs, o_ref, km_ref = refs[:items * npp], refs[items * npp], refs[items * npp + 1]
    for it in range(items):
        acc = jnp.sum(page_refs[it * npp][0].astype(F32), axis=0)
        for p in range(1, npp):
            acc = acc + jnp.sum(page_refs[it * npp + p][0].astype(F32), axis=0)
        km_ref[it] = acc / blk

    k32 = k_ref[0]
    kb = k32.astype(BF16)
    v_t = v_ref[0].T.astype(BF16)
    nblk_pad = -(-nblk // V7X_SUBLANES) * V7X_SUBLANES
    means = [jnp.mean(k32[b * blk:(b + 1) * blk], axis=0, keepdims=True) for b in range(nblk)]
    means += [jnp.zeros_like(means[0])] * (nblk_pad - nblk)
    kmean = jnp.concatenate(means, axis=0)
    key_i = lax.broadcasted_iota(jnp.int32, (blk, blk), 0)
    qry_i = lax.broadcasted_iota(jnp.int32, (blk, blk), 1)
    causal = key_i <= qry_i
    neg_inf = jnp.float32(-jnp.inf)
    contract_last = (((1,), (1,)), ((), ()))
    exp2_scale = scale * LOG2_E

    for c in range(nblk):
        q32 = q_ref[0, c * blk:(c + 1) * blk, :]
        qb = q32.astype(BF16)
        sels = [None] * c
        if c > n_sel:
            gate = lax.dot_general(kmean, q32, contract_last, precision=lax.Precision.HIGHEST,
                                   preferred_element_type=F32)
            g = [gate[b:b + 1, :] for b in range(c)]
            for b in range(c):
                cnt = jnp.zeros((1, blk), jnp.int32)
                for b2 in range(c):
                    if b2 == b:
                        continue
                    beats = (g[b2] >= g[b]) if b2 < b else (g[b2] > g[b])
                    cnt = cnt + beats.astype(jnp.int32)
                sels[b] = cnt < n_sel
        s_blocks = []
        for b in range(c + 1):
            s = lax.dot_general(kb[b * blk:(b + 1) * blk], qb, contract_last,
                                preferred_element_type=F32)
            if b == c:
                s = jnp.where(causal, s, neg_inf)
            elif sels[b] is not None:
                s = jnp.where(sels[b], s, neg_inf)
            s_blocks.append(s)
        m = jnp.max(s_blocks[c], axis=0, keepdims=True)
        for b in range(c):
            m = jnp.maximum(m, jnp.max(s_blocks[b], axis=0, keepdims=True))
        l = jnp.zeros((1, blk), F32)
        o_t = jnp.zeros((v_t.shape[0], blk), F32)
        for b in range(c + 1):
            p = jnp.exp2((s_blocks[b] - m) * exp2_scale)
            l = l + jnp.sum(p, axis=0, keepdims=True)
            o_t = o_t + jnp.dot(v_t[:, b * blk:(b + 1) * blk], p.astype(BF16),
                                preferred_element_type=F32)
        o_ref[0, c * blk:(c + 1) * blk, :] = (o_t / l).T.astype(o_ref.dtype)


def _moba_seq(q, k, v, n_heads, cache_k, page_table):
    b, t, d = q.shape
    hd = d // n_heads
    assert t % MOBA_BLOCK == 0 and hd % V7X_LANES == 0
    nblk = t // MOBA_BLOCK
    _, page, c_heads, c_hd = cache_k.shape
    s, npages = page_table.shape
    npp = MOBA_BLOCK // page
    n_items = s * (npages // npp)
    steps = b * n_heads
    items = -(-n_items // steps)

    def page_spec(it, p):
        def index(bi, hi, pt):
            item = jnp.minimum((bi * n_heads + hi) * items + it, n_items - 1)
            return (pt[item * npp + p], 0, 0, 0)
        return pl.BlockSpec((1, page, c_heads, c_hd), index)

    spec = pl.BlockSpec((1, t, hd), lambda bi, hi, pt: (bi, 0, hi))
    grid_spec = pltpu.PrefetchScalarGridSpec(
        num_scalar_prefetch=1,
        grid=(b, n_heads),
        in_specs=[spec, spec, spec] + [page_spec(it, p) for it in range(items) for p in range(npp)],
        out_specs=[spec, pl.BlockSpec((items, c_heads, c_hd),
                                      lambda bi, hi, pt: (bi * n_heads + hi, 0, 0))],
    )
    block_bytes = (3 * _nbytes((t, hd), F32) + _nbytes((t, hd), BF16)
                   + items * npp * _nbytes((page, c_heads, c_hd), cache_k.dtype)
                   + items * _nbytes((c_heads, c_hd), F32))
    o, kmean = pl.pallas_call(
        functools.partial(_moba_seq_kernel, nblk=nblk, blk=MOBA_BLOCK,
                          n_sel=min(MOBA_TOP_K, nblk - 1), scale=hd ** -0.5, items=items, npp=npp),
        grid_spec=grid_spec,
        out_shape=[jax.ShapeDtypeStruct((b, t, d), BF16),
                   jax.ShapeDtypeStruct((steps * items, c_heads, c_hd), F32)],
        compiler_params=_params(("parallel", "parallel"), block_bytes),
        name="moba_seq",
    )(page_table.reshape(-1), q, k, v, *([cache_k] * (items * npp)))
    return o, kmean[:n_items].reshape(s, npages // npp, c_heads, c_hd)


def _block_select_kernel(q_ref, km_ref, sel_ref, *, n_sel):
    q = q_ref[0]
    km = km_ref[0]
    nb = km.shape[0]
    g = jnp.sum(km * q[None], axis=-1)
    blk_i = lax.broadcasted_iota(jnp.int32, g.shape, 0)
    rows = []
    for _ in range(n_sel):
        m = jnp.max(g, axis=0, keepdims=True)
        idx = jnp.min(jnp.where(g == m, blk_i, nb), axis=0, keepdims=True)
        rows.append(idx)
        g = jnp.where(blk_i == idx, -jnp.inf, g)
    sel_ref[0] = jnp.concatenate(rows, axis=0)


def _block_select(q, kmean, n_sel):
    s, nb, n_heads, hd = kmean.shape
    return pl.pallas_call(
        functools.partial(_block_select_kernel, n_sel=n_sel),
        grid=(s,),
        in_specs=[pl.BlockSpec((1, n_heads, hd), lambda si: (si, 0, 0)),
                  pl.BlockSpec((1, nb, n_heads, hd), lambda si: (si, 0, 0, 0))],
        out_specs=pl.BlockSpec((1, n_sel, n_heads), lambda si: (si, 0, 0)),
        out_shape=jax.ShapeDtypeStruct((s, n_sel, n_heads), jnp.int32),
        name="block_select",
    )(q, kmean)


def _moba_step_kernel(pt_ref, sel_ref, q_ref, kn_ref, vn_ref, ck_hbm, cv_hbm, o_ref, kbuf, vbuf,
                      sems, *, n_sel, npp, npages, scale):
    si, ns = pl.program_id(0), pl.num_programs(0)
    n_heads = q_ref.shape[1]
    page = ck_hbm.shape[1]
    group = n_heads // 2

    def gather(seq, grp):
        copies = []
        for hl in range(group):
            h = grp * group + hl
            for r in range(n_sel):
                blk = sel_ref[(seq * n_sel + r) * n_heads + h]
                for p in range(npp):
                    pg = pt_ref[seq * npages + blk * npp + p]
                    rows = pl.ds((r * npp + p) * page, page)
                    copies.append(pltpu.make_async_copy(
                        ck_hbm.at[pg, :, h, :], kbuf.at[grp, hl, rows, :], sems.at[0, grp]))
                    copies.append(pltpu.make_async_copy(
                        cv_hbm.at[pg, :, h, :], vbuf.at[grp, hl, rows, :], sems.at[1, grp]))
        return copies

    def attend(grp):
        for hl in range(group):
            h = grp * group + hl
            q = q_ref[0, h:h + 1, :]
            s = jnp.sum(kbuf[grp, hl] * q, axis=-1, keepdims=True) * scale
            s_own = jnp.sum(kn_ref[0, h:h + 1, :] * q, axis=-1, keepdims=True) * scale
            m = jnp.maximum(jnp.max(s, axis=0, keepdims=True), s_own)
            p = jnp.exp(s - m)
            p_own = jnp.exp(s_own - m)
            l = jnp.sum(p, axis=0, keepdims=True) + p_own
            o = jnp.sum(p * vbuf[grp, hl], axis=0, keepdims=True) + p_own * vn_ref[0, h:h + 1, :]
            o_ref[0, h:h + 1, :] = (o / l).astype(o_ref.dtype)

    @pl.when(si == 0)
    def _():
        for grp in range(2):
            for cp in gather(si, grp):
                cp.start()

    for grp in range(2):
        for cp in gather(si, grp):
            cp.wait()
        attend(grp)

        @pl.when(si + 1 < ns)
        def _():
            for cp in gather(si + 1, grp):
                cp.start()


def _moba_step(q, k_new, v_new, cache_k, cache_v, page_table, sel):
    s, n_heads, hd = q.shape
    _, page, _, _ = cache_k.shape
    npages = page_table.shape[1]
    n_sel = sel.shape[1]
    npp = MOBA_BLOCK // page
    assert n_heads % 2 == 0
    keys = n_sel * MOBA_BLOCK
    row = pl.BlockSpec((1, n_heads, hd), lambda si, pt, sl: (si, 0, 0))
    grid_spec = pltpu.PrefetchScalarGridSpec(
        num_scalar_prefetch=2,
        grid=(s,),
        in_specs=[row, row, row, pl.BlockSpec(memory_space=pl.ANY),
                  pl.BlockSpec(memory_space=pl.ANY)],
        out_specs=row,
        scratch_shapes=[pltpu.VMEM((2, n_heads // 2, keys, hd), cache_k.dtype),
                        pltpu.VMEM((2, n_heads // 2, keys, hd), cache_v.dtype),
                        pltpu.SemaphoreType.DMA((2, 2))],
    )
    scratch_bytes = 2 * _nbytes((n_heads, keys, hd), cache_k.dtype)
    out = pl.pallas_call(
        functools.partial(_moba_step_kernel, n_sel=n_sel, npp=npp, npages=npages, scale=hd ** -0.5),
        grid_spec=grid_spec,
        out_shape=jax.ShapeDtypeStruct((s, n_heads, hd), F32),
        compiler_params=_params(("arbitrary",), 4 * _nbytes((n_heads, hd), F32), scratch_bytes),
        name="moba_step",
    )(page_table.reshape(-1), sel.reshape(-1), q, k_new, v_new, cache_k, cache_v)
    return out.reshape(s, n_heads * hd)


def _swiglu_block(hp, hs, norm_gain, w_gate, w_up, w_down, layer, tag):
    (xp,) = _rmsnorm(hp, [norm_gain], [BF16], f"ffn_norm_p{tag}")
    (xs,) = _rmsnorm(hs, [norm_gain], [BF16], f"ffn_norm_s{tag}")
    (ap,), (as_,) = _matmul(xp, xs, [(w_gate, layer), (w_up, layer)], _ep_swiglu, [BF16],
                            tm=1024, tn=256, name=f"ffn_gate_up{tag}")
    wd = [(w_down, layer)]
    (pp,), (ps,) = _matmul(ap, as_, wd, _ep_identity, [F32], tm=512, tn=512, k_part=(0, 2),
                           name=f"ffn_down_a{tag}")
    (hp,), (hs,) = _matmul(ap, as_, wd, _ep_residual_partial, [F32], [pp, hp], [ps, hs],
                           tm=512, tn=512, k_part=(1, 2), name=f"ffn_down_b{tag}")
    return hp, hs


def kernel(x_prompt, x_sample, cache_k, cache_v, page_table, state_conv, state_h, mixer_norm, lru_w_x, lru_w_y, lru_conv_w, lru_conv_b, lru_w_a, lru_b_a, lru_w_i, lru_b_i, lru_lambda, lru_w_out, kv_norm, w_k, w_v, w_q, w_o, ffn_norm, w_gate, w_up, w_down, final_norm):
    bp, tp, d = x_prompt.shape
    bs, ts, _ = x_sample.shape
    n_heads, hd = cache_k.shape[2], cache_k.shape[3]
    page = cache_k.shape[1]
    past_len = page_table.shape[1] * page
    depth = mixer_norm.shape[0]
    n_a = lru_w_x.shape[0]
    assert ts == 1, "the decode group advances one position per step"
    assert MOBA_BLOCK % page == 0 and past_len % MOBA_BLOCK == 0
    n_sel_s = min(MOBA_TOP_K, past_len // MOBA_BLOCK)
    assert past_len // MOBA_BLOCK >= n_sel_s

    hp = x_prompt.reshape(bp * tp, d)
    hs = x_sample.reshape(bs * ts, d)
    conv_p, conv_s, hl_p, hl_s = [], [], [], []
    k_p = v_p = k_s = v_s = None

    for layer in range(depth):
        (xp,) = _rmsnorm(hp, [mixer_norm[layer]], [BF16], f"mixer_norm_p{layer}")
        (xs,) = _rmsnorm(hs, [mixer_norm[layer]], [BF16], f"mixer_norm_s{layer}")
        if layer < n_a:
            a = layer
            lru = (lru_conv_w[a], lru_conv_b[a], lru_w_a[a], lru_b_a[a], lru_w_i[a], lru_b_i[a],
                   lru_lambda[a])
            cwid = lru_conv_w.shape[1]
            (xr_p, gate_p), (xr_s, gate_s) = _matmul(
                xp, xs, [(lru_w_x, a), (lru_w_y, a)], _ep_x_gelu, [F32, F32], tm=1024, tn=256,
                name=f"lru_in{layer}")
            w = xr_p.shape[1]
            xr3 = xr_p.reshape(bp, tp, w)
            hg_p, hlast_p = _rglru_seq(xr3, gate_p.reshape(bp, tp, w),
                                       jnp.zeros((bp, cwid - 1, w), F32), jnp.zeros((bp, w), F32),
                                       *lru, first_pos_zero=True)
            hg_s, hlast_s, nbuf = _rglru_step(xr_s, gate_s, state_conv[a], state_h[a], *lru,
                                              first_pos_zero=(past_len == 0))
            (hp,), (hs,) = _matmul(hg_p.reshape(bp * tp, w), hg_s, [(lru_w_out, a)], _ep_residual,
                                   [F32], [hp], [hs], tm=1024, tn=512, name=f"lru_out{layer}")
            conv_p.append(xr3[:, tp - (cwid - 1):, :])
            conv_s.append(nbuf)
            hl_p.append(hlast_p)
            hl_s.append(hlast_s)
        else:
            b = layer - n_a
            (q_p,), (q_s,) = _matmul(xp, xs, [(w_q, b)], _ep_identity, [F32], tm=1024, tn=512,
                                     name=f"attn_q{layer}")
            o_p, kmean_s = _moba_seq(q_p.reshape(bp, tp, d), k_p.reshape(bp, tp, d),
                                     v_p.reshape(bp, tp, d), n_heads, cache_k, page_table)
            q3 = q_s.reshape(bs, n_heads, hd)
            sel = _block_select(q3, kmean_s, n_sel_s)
            o_s = _moba_step(q3, k_s.reshape(bs, n_heads, hd), v_s.reshape(bs, n_heads, hd),
                             cache_k, cache_v, page_table, sel)
            (hp,), (hs,) = _matmul(o_p.reshape(bp * tp, d), o_s.astype(BF16), [(w_o, b)],
                                   _ep_residual, [F32], [hp], [hs], tm=1024, tn=512,
                                   name=f"attn_o{layer}")

        hp, hs = _swiglu_block(hp, hs, ffn_norm[layer], w_gate, w_up, w_down, layer, str(layer))

        if layer == n_a - 1:
            (xp,) = _rmsnorm(hp, [kv_norm], [BF16], "kv_norm_p")
            (xs,) = _rmsnorm(hs, [kv_norm], [BF16], "kv_norm_s")
            (k_p, v_p), (k_s, v_s) = _matmul(
                xp, xs, [(w_k.reshape(1, *w_k.shape), 0), (w_v.reshape(1, *w_v.shape), 0)],
                _ep_identity, [F32, F32], tm=1024, tn=256, name="kv")

    (y_p,) = _rmsnorm(hp, [final_norm], [F32], "final_norm_p")
    (y_s,) = _rmsnorm(hs, [final_norm], [F32], "final_norm_s")
    return (y_p.reshape(bp, tp, d), y_s.reshape(bs, ts, d),
            k_p.reshape(bp, tp, n_heads, hd), v_p.reshape(bp, tp, n_heads, hd),
            k_s.reshape(bs, ts, n_heads, hd), v_s.reshape(bs, ts, n_heads, hd),
            jnp.stack(conv_p), jnp.stack(conv_s), jnp.stack(hl_p), jnp.stack(hl_s))
```

```python
import functools

import jax
import jax.numpy as jnp
from jax import lax
from jax.experimental import pallas as pl
from jax.experimental.pallas import tpu as pltpu

F32 = jnp.float32
BF16 = jnp.bfloat16

RMS_EPS = 1e-6
LRU_C = 8.0
MOBA_BLOCK = 256
MOBA_TOP_K = 3
LOG2_E = 1.4426950408889634

V7X_VMEM_BYTES = 64 * 1024 * 1024
V7X_SUBLANES = 8
V7X_LANES = 128
_VMEM_INTERNAL_BYTES = 8 * 1024 * 1024


def _pick_tile(dim, target, align):
    best = None
    d = align
    while d <= min(dim, target):
        if dim % d == 0:
            best = d
        d += align
    return best if best is not None else dim


def _nbytes(shape, dtype):
    n = 1
    for s in shape:
        n *= s
    return n * jnp.dtype(dtype).itemsize


def _params(semantics, block_bytes, scratch_bytes=0):
    limit = 2 * block_bytes + scratch_bytes + _VMEM_INTERNAL_BYTES
    limit = min(max(limit, 16 * 1024 * 1024), V7X_VMEM_BYTES - 4 * 1024 * 1024)
    return pltpu.CompilerParams(dimension_semantics=semantics, vmem_limit_bytes=int(limit))


def _rmsnorm_kernel(x_ref, *refs, n_out):
    g_refs, o_refs = refs[:n_out], refs[n_out:]
    x = x_ref[...].astype(F32)
    y = x * lax.rsqrt(jnp.mean(x * x, axis=-1, keepdims=True) + RMS_EPS)
    for g_ref, o_ref in zip(g_refs, o_refs):
        o_ref[...] = (y * g_ref[...].astype(F32)).astype(o_ref.dtype)


def _rmsnorm(x, gains, out_dtypes, name):
    m, d = x.shape
    tm = _pick_tile(m, 256, V7X_SUBLANES)
    n_out = len(gains)
    block_bytes = _nbytes((tm, d), x.dtype) + sum(_nbytes((tm, d), dt) for dt in out_dtypes)
    outs = pl.pallas_call(
        functools.partial(_rmsnorm_kernel, n_out=n_out),
        grid=(m // tm,),
        in_specs=[pl.BlockSpec((tm, d), lambda i: (i, 0))]
        + [pl.BlockSpec((1, d), lambda i: (0, 0))] * n_out,
        out_specs=[pl.BlockSpec((tm, d), lambda i: (i, 0))] * n_out,
        out_shape=[jax.ShapeDtypeStruct((m, d), dt) for dt in out_dtypes],
        compiler_params=_params(("parallel",), block_bytes),
        name=name,
    )(x, *[g.reshape(1, d) for g in gains])
    return outs


def _matmul_kernel(*refs, n_w, n_extra, n_out, epilogue):
    ap_ref, as_ref = refs[0], refs[1]
    pos = 2
    w_refs = refs[pos:pos + n_w]
    pos += n_w
    ep_refs = refs[pos:pos + n_extra]
    pos += n_extra
    es_refs = refs[pos:pos + n_extra]
    pos += n_extra
    op_refs = refs[pos:pos + n_out]
    pos += n_out
    os_refs = refs[pos:pos + n_out]
    pos += n_out
    wb_refs = refs[pos:]

    def apply(a_ref, e_refs, o_refs):
        a = a_ref[...]
        accs = [jnp.dot(a, wb[...], preferred_element_type=F32) for wb in wb_refs]
        outs = epilogue(*accs, *[e[...] for e in e_refs])
        for o_ref, val in zip(o_refs, outs):
            o_ref[...] = val.astype(o_ref.dtype)

    @pl.when(pl.program_id(1) == 0)
    def _():
        for w_ref, wb_ref in zip(w_refs, wb_refs):
            wb_ref[...] = w_ref[...].astype(BF16)
        apply(as_ref, es_refs, os_refs)

    apply(ap_ref, ep_refs, op_refs)


def _matmul(a_p, a_s, ws, epilogue, out_dtypes, extras_p=(), extras_s=(), *, tm, tn,
            k_part=(0, 1), name):
    m, kdim = a_p.shape
    s = a_s.shape[0]
    n = ws[0][0].shape[2]
    kp, nkp = k_part
    assert kdim % nkp == 0
    tk = kdim // nkp
    tm = _pick_tile(m, tm, V7X_SUBLANES)
    n_w, n_extra, n_out = len(ws), len(extras_p), len(out_dtypes)
    assert len(extras_s) == n_extra
    tn = min(tn, n)
    assert tn % V7X_LANES == 0

    def w_spec(layer):
        return pl.BlockSpec((None, tk, tn), lambda j, i: (layer, kp, j))

    p_tile = pl.BlockSpec((tm, tn), lambda j, i: (i, j))
    s_tile = pl.BlockSpec((s, tn), lambda j, i: (0, j))
    block_bytes = (_nbytes((tm + s, tk), BF16) + n_w * _nbytes((tk, tn), F32)
                   + sum(_nbytes((tm + s, tn), e.dtype) for e in extras_p)
                   + sum(_nbytes((tm + s, tn), dt) for dt in out_dtypes))
    outs = pl.pallas_call(
        functools.partial(_matmul_kernel, n_w=n_w, n_extra=n_extra, n_out=n_out, epilogue=epilogue),
        grid=(pl.cdiv(n, tn), m // tm),
        in_specs=[pl.BlockSpec((tm, tk), lambda j, i: (i, kp)),
                  pl.BlockSpec((s, tk), lambda j, i: (0, kp))]
        + [w_spec(layer) for _, layer in ws]
        + [p_tile] * n_extra + [s_tile] * n_extra,
        out_specs=[p_tile] * n_out + [s_tile] * n_out,
        out_shape=[jax.ShapeDtypeStruct((m, n), dt) for dt in out_dtypes]
        + [jax.ShapeDtypeStruct((s, n), dt) for dt in out_dtypes],
        scratch_shapes=[pltpu.VMEM((tk, tn), BF16) for _ in ws],
        compiler_params=_params(("parallel", "arbitrary"), block_bytes,
                                n_w * _nbytes((tk, tn), BF16)),
        name=name,
    )(a_p, a_s, *[w for w, _ in ws], *extras_p, *extras_s)
    return outs[:n_out], outs[n_out:]


def _ep_identity(*accs):
    return accs


def _ep_x_gelu(acc_x, acc_y):
    return acc_x, jax.nn.gelu(acc_y)


def _ep_residual(acc, res):
    return (res + acc,)


def _ep_residual_partial(acc, partial, res):
    return (res + (partial + acc),)


def _ep_swiglu(acc_gate, acc_up):
    return (jax.nn.silu(acc_gate) * acc_up,)


def _lru_gates(xc, head, wa_ref, ba_ref, wi_ref, bi_ref, lam):
    xb = xc.astype(BF16)
    r = jax.nn.sigmoid(jnp.dot(xb, wa_ref[head], preferred_element_type=F32) + ba_ref[head])
    i = jax.nn.sigmoid(jnp.dot(xb, wi_ref[head], preferred_element_type=F32) + bi_ref[head])
    log_a = LRU_C * r * jax.nn.log_sigmoid(lam)
    a = jnp.exp(log_a)
    mult = jnp.sqrt(-jnp.tanh(log_a) * (a * a + 1.0))
    return a, mult, i


def _rglru_seq_kernel(xr_ref, gate_ref, cbuf_ref, h0_ref, cw_ref, cb_ref, wa_ref, ba_ref, wi_ref,
                      bi_ref, lam_ref, hg_ref, hlast_ref, xp_scr, a_scr, u_scr, hs_scr, h_scr, *,
                      tt, nh, dh, cwid, first_pos_zero):
    t = pl.program_id(2)
    halo = cwid - 1
    base = V7X_SUBLANES - halo

    @pl.when(t == 0)
    def _():
        xp_scr[base:V7X_SUBLANES, :] = cbuf_ref[0].astype(F32)
        h_scr[...] = h0_ref[0].astype(F32)

    xp_scr[V7X_SUBLANES:V7X_SUBLANES + tt, :] = xr_ref[0]
    xc = cb_ref[...] + cw_ref[0:1, :] * xp_scr[base:base + tt, :]
    for j in range(1, cwid):
        xc = xc + cw_ref[j:j + 1, :] * xp_scr[base + j:base + j + tt, :]
    xp_scr[base:V7X_SUBLANES, :] = xp_scr[base + tt:V7X_SUBLANES + tt, :]

    for hd in range(nh):
        sl = slice(hd * dh, (hd + 1) * dh)
        xh = xc[:, sl]
        a, mult, i = _lru_gates(xh, hd, wa_ref, ba_ref, wi_ref, bi_ref, lam_ref[:, sl])
        if first_pos_zero:
            row = lax.broadcasted_iota(jnp.int32, mult.shape, 0)
            mult = jnp.where((row == 0) & (t == 0), 1.0, mult)
        a_scr[:, sl] = a
        u_scr[:, sl] = mult * i * xh

    def step(s, h):
        h = a_scr[pl.ds(s, 1), :] * h + u_scr[pl.ds(s, 1), :]
        hs_scr[pl.ds(s, 1), :] = h
        return h

    h = lax.fori_loop(0, tt, step, h_scr[...], unroll=8)
    h_scr[...] = h
    hg_ref[0] = (hs_scr[...] * gate_ref[0]).astype(hg_ref.dtype)
    hlast_ref[0] = h


def _rglru_seq(xr, gate, conv_buf, h0, conv_w, conv_b, w_a, b_a, w_i, b_i, lam, *, first_pos_zero):
    b, t, w = xr.shape
    nheads, dh = w_a.shape[0], w_a.shape[1]
    cwid = conv_w.shape[0]
    wc = _pick_tile(w, 1024, dh)
    nh = wc // dh
    tt = _pick_tile(t, 256, V7X_SUBLANES)
    blk = lambda bi_, ci, ti: (bi_, ti, ci)
    per_bc = lambda bi_, ci, ti: (bi_, 0, ci)
    chan = lambda bi_, ci, ti: (0, ci)
    head = lambda bi_, ci, ti: (ci, 0, 0)
    block_bytes = (2 * _nbytes((tt, wc), F32) + _nbytes((tt, wc), BF16)
                   + 2 * _nbytes((nh, dh, dh), BF16) + 16 * _nbytes((1, wc), F32))
    scratch_bytes = 4 * _nbytes((tt + V7X_SUBLANES, wc), F32)
    hg, hlast = pl.pallas_call(
        functools.partial(_rglru_seq_kernel, tt=tt, nh=nh, dh=dh, cwid=cwid,
                          first_pos_zero=first_pos_zero),
        grid=(b, w // wc, t // tt),
        in_specs=[
            pl.BlockSpec((1, tt, wc), blk),
            pl.BlockSpec((1, tt, wc), blk),
            pl.BlockSpec((1, cwid - 1, wc), per_bc),
            pl.BlockSpec((1, 1, wc), per_bc),
            pl.BlockSpec((cwid, wc), chan),
            pl.BlockSpec((1, wc), chan),
            pl.BlockSpec((nh, dh, dh), head),
            pl.BlockSpec((nh, 1, dh), head),
            pl.BlockSpec((nh, dh, dh), head),
            pl.BlockSpec((nh, 1, dh), head),
            pl.BlockSpec((1, wc), chan),
        ],
        out_specs=[pl.BlockSpec((1, tt, wc), blk), pl.BlockSpec((1, 1, wc), per_bc)],
        out_shape=[jax.ShapeDtypeStruct((b, t, w), BF16), jax.ShapeDtypeStruct((b, 1, w), F32)],
        scratch_shapes=[
            pltpu.VMEM((tt + V7X_SUBLANES, wc), F32),
            pltpu.VMEM((tt, wc), F32),
            pltpu.VMEM((tt, wc), F32),
            pltpu.VMEM((tt, wc), F32),
            pltpu.VMEM((1, wc), F32),
        ],
        compiler_params=_params(("parallel", "parallel", "arbitrary"), block_bytes, scratch_bytes),
        name="rglru_seq",
    )(xr, gate, conv_buf, h0.reshape(b, 1, w), conv_w, conv_b.reshape(1, w),
      w_a.astype(BF16), b_a.reshape(nheads, 1, dh), w_i.astype(BF16), b_i.reshape(nheads, 1, dh),
      lam.reshape(1, w))
    return hg, hlast.reshape(b, w)


def _rglru_step_kernel(xr_ref, gate_ref, cbuf_ref, h0_ref, cw_ref, cb_ref, wa_ref, ba_ref, wi_ref,
                       bi_ref, lam_ref, hg_ref, h_ref, nbuf_ref, *, nheads, dh, cwid,
                       first_pos_zero):
    xr = xr_ref[...]
    xc = cb_ref[...] + cw_ref[cwid - 1:cwid, :] * xr
    for j in range(cwid - 1):
        xc = xc + cw_ref[j:j + 1, :] * cbuf_ref[j]
    for j in range(cwid - 2):
        nbuf_ref[j] = cbuf_ref[j + 1]
    nbuf_ref[cwid - 2] = xr
    for hd in range(nheads):
        sl = slice(hd * dh, (hd + 1) * dh)
        xh = xc[:, sl]
        a, mult, i = _lru_gates(xh, hd, wa_ref, ba_ref, wi_ref, bi_ref, lam_ref[:, sl])
        if first_pos_zero:
            mult = jnp.ones_like(mult)
        h = a * h0_ref[:, sl] + mult * i * xh
        h_ref[:, sl] = h
        hg_ref[:, sl] = (h * gate_ref[:, sl]).astype(hg_ref.dtype)


def _rglru_step(xr, gate, conv_buf, h0, conv_w, conv_b, w_a, b_a, w_i, b_i, lam, *, first_pos_zero):
    s, w = xr.shape
    nheads, dh = w_a.shape[0], w_a.shape[1]
    cwid = conv_w.shape[0]
    hg, h, nbuf = pl.pallas_call(
        functools.partial(_rglru_step_kernel, nheads=nheads, dh=dh, cwid=cwid,
                          first_pos_zero=first_pos_zero),
        out_shape=[jax.ShapeDtypeStruct((s, w), BF16), jax.ShapeDtypeStruct((s, w), F32),
                   jax.ShapeDtypeStruct((cwid - 1, s, w), F32)],
        name="rglru_step",
    )(xr, gate, jnp.swapaxes(conv_buf, 0, 1), h0, conv_w, conv_b.reshape(1, w),
      w_a.astype(BF16), b_a.reshape(nheads, 1, dh), w_i.astype(BF16), b_i.reshape(nheads, 1, dh),
      lam.reshape(1, w))
    return hg, h, jnp.swapaxes(nbuf, 0, 1)


def _moba_seq_kernel(pt_ref, q_ref, k_ref, v_ref, *refs, nblk, blk, n_sel, scale, items, npp):
    page_refs, o_ref, km_ref = refs[:items * npp], refs[items * npp], refs[items * npp + 1]
    for it in range(items):
        acc = jnp.sum(page_refs[it * npp][0].astype(F32), axis=0)
        for p in range(1, npp):
            acc = acc + jnp.sum(page_refs[it * npp + p][0].astype(F32), axis=0)
        km_ref[it] = acc / blk

    k32 = k_ref[0]
    kb = k32.astype(BF16)
    v_t = v_ref[0].T.astype(BF16)
    nblk_pad = -(-nblk // V7X_SUBLANES) * V7X_SUBLANES
    means = [jnp.mean(k32[b * blk:(b + 1) * blk], axis=0, keepdims=True) for b in range(nblk)]
    means += [jnp.zeros_like(means[0])] * (nblk_pad - nblk)
    kmean = jnp.concatenate(means, axis=0)
    key_i = lax.broadcasted_iota(jnp.int32, (blk, blk), 0)
    qry_i = lax.broadcasted_iota(jnp.int32, (blk, blk), 1)
    causal = key_i <= qry_i
    neg_inf = jnp.float32(-jnp.inf)
    contract_last = (((1,), (1,)), ((), ()))
    exp2_scale = scale * LOG2_E

    for c in range(nblk):
        q32 = q_ref[0, c * blk:(c + 1) * blk, :]
        qb = q32.astype(BF16)
        sels = [None] * c
        if c > n_sel:
            gate = lax.dot_general(kmean, q32, contract_last, precision=lax.Precision.HIGHEST,
                                   preferred_element_type=F32)
            g = [gate[b:b + 1, :] for b in range(c)]
            for b in range(c):
                cnt = jnp.zeros((1, blk), jnp.int32)
                for b2 in range(c):
                    if b2 == b:
                        continue
                    beats = (g[b2] >= g[b]) if b2 < b else (g[b2] > g[b])
                    cnt = cnt + beats.astype(jnp.int32)
                sels[b] = cnt < n_sel
        s_blocks = []
        for b in range(c + 1):
            s = lax.dot_general(kb[b * blk:(b + 1) * blk], qb, contract_last,
                                preferred_element_type=F32)
            if b == c:
                s = jnp.where(causal, s, neg_inf)
            elif sels[b] is not None:
                s = jnp.where(sels[b], s, neg_inf)
            s_blocks.append(s)
        m = jnp.max(s_blocks[c], axis=0, keepdims=True)
        for b in range(c):
            m = jnp.maximum(m, jnp.max(s_blocks[b], axis=0, keepdims=True))
        l = jnp.zeros((1, blk), F32)
        o_t = jnp.zeros((v_t.shape[0], blk), F32)
        for b in range(c + 1):
            p = jnp.exp2((s_blocks[b] - m) * exp2_scale)
            l = l + jnp.sum(p, axis=0, keepdims=True)
            o_t = o_t + jnp.dot(v_t[:, b * blk:(b + 1) * blk], p.astype(BF16),
                                preferred_element_type=F32)
        o_ref[0, c * blk:(c + 1) * blk, :] = (o_t / l).T.astype(o_ref.dtype)


def _moba_seq(q, k, v, n_heads, cache_k, page_table):
    b, t, d = q.shape
    hd = d // n_heads
    assert t % MOBA_BLOCK == 0 and hd % V7X_LANES == 0
    nblk = t // MOBA_BLOCK
    _, page, c_heads, c_hd = cache_k.shape
    s, npages = page_table.shape
    npp = MOBA_BLOCK // page
    n_items = s * (npages // npp)
    steps = b * n_heads
    items = -(-n_items // steps)

    def page_spec(it, p):
        def index(bi, hi, pt):
            item = jnp.minimum((bi * n_heads + hi) * items + it, n_items - 1)
            return (pt[item * npp + p], 0, 0, 0)
        return pl.BlockSpec((1, page, c_heads, c_hd), index)

    spec = pl.BlockSpec((1, t, hd), lambda bi, hi, pt: (bi, 0, hi))
    grid_spec = pltpu.PrefetchScalarGridSpec(
        num_scalar_prefetch=1,
        grid=(b, n_heads),
        in_specs=[spec, spec, spec] + [page_spec(it, p) for it in range(items) for p in range(npp)],
        out_specs=[spec, pl.BlockSpec((items, c_heads, c_hd),
                                      lambda bi, hi, pt: (bi * n_heads + hi, 0, 0))],
    )
    block_bytes = (3 * _nbytes((t, hd), F32) + _nbytes((t, hd), BF16)
                   + items * npp * _nbytes((page, c_heads, c_hd), cache_k.dtype)
                   + items * _nbytes((c_heads, c_hd), F32))
    o, kmean = pl.pallas_call(
        functools.partial(_moba_seq_kernel, nblk=nblk, blk=MOBA_BLOCK,
                          n_sel=min(MOBA_TOP_K, nblk - 1), scale=hd ** -0.5, items=items, npp=npp),
        grid_spec=grid_spec,
        out_shape=[jax.ShapeDtypeStruct((b, t, d), BF16),
                   jax.ShapeDtypeStruct((steps * items, c_heads, c_hd), F32)],
        compiler_params=_params(("parallel", "parallel"), block_bytes),
        name="moba_seq",
    )(page_table.reshape(-1), q, k, v, *([cache_k] * (items * npp)))
    return o, kmean[:n_items].reshape(s, npages // npp, c_heads, c_hd)


def _block_select_kernel(q_ref, km_ref, sel_ref, *, n_sel):
    q = q_ref[0]
    km = km_ref[0]
    nb = km.shape[0]
    g = jnp.sum(km * q[None], axis=-1)
    blk_i = lax.broadcasted_iota(jnp.int32, g.shape, 0)
    rows = []
    for _ in range(n_sel):
        m = jnp.max(g, axis=0, keepdims=True)
        idx = jnp.min(jnp.where(g == m, blk_i, nb), axis=0, keepdims=True)
        rows.append(idx)
        g = jnp.where(blk_i == idx, -jnp.inf, g)
    sel_ref[0] = jnp.concatenate(rows, axis=0)


def _block_select(q, kmean, n_sel):
    s, nb, n_heads, hd = kmean.shape
    return pl.pallas_call(
        functools.partial(_block_select_kernel, n_sel=n_sel),
        grid=(s,),
        in_specs=[pl.BlockSpec((1, n_heads, hd), lambda si: (si, 0, 0)),
                  pl.BlockSpec((1, nb, n_heads, hd), lambda si: (si, 0, 0, 0))],
        out_specs=pl.BlockSpec((1, n_sel, n_heads), lambda si: (si, 0, 0)),
        out_shape=jax.ShapeDtypeStruct((s, n_sel, n_heads), jnp.int32),
        name="block_select",
    )(q, kmean)


def _moba_step_kernel(pt_ref, sel_ref, q_ref, kn_ref, vn_ref, ck_hbm, cv_hbm, o_ref, kbuf, vbuf,
                      sems, *, n_sel, npp, npages, scale):
    si, ns = pl.program_id(0), pl.num_programs(0)
    n_heads = q_ref.shape[1]
    page = ck_hbm.shape[1]
    group = n_heads // 2

    def gather(seq, grp):
        copies = []
        for hl in range(group):
            h = grp * group + hl
            for r in range(n_sel):
                blk = sel_ref[(seq * n_sel + r) * n_heads + h]
                for p in range(npp):
                    pg = pt_ref[seq * npages + blk * npp + p]
                    rows = pl.ds((r * npp + p) * page, page)
                    copies.append(pltpu.make_async_copy(
                        ck_hbm.at[pg, :, h, :], kbuf.at[grp, hl, rows, :], sems.at[0, grp]))
                    copies.append(pltpu.make_async_copy(
                        cv_hbm.at[pg, :, h, :], vbuf.at[grp, hl, rows, :], sems.at[1, grp]))
        return copies

    def attend(grp):
        for hl in range(group):
            h = grp * group + hl
            q = q_ref[0, h:h + 1, :]
            s = jnp.sum(kbuf[grp, hl] * q, axis=-1, keepdims=True) * scale
            s_own = jnp.sum(kn_ref[0, h:h + 1, :] * q, axis=-1, keepdims=True) * scale
            m = jnp.maximum(jnp.max(s, axis=0, keepdims=True), s_own)
            p = jnp.exp(s - m)
            p_own = jnp.exp(s_own - m)
            l = jnp.sum(p, axis=0, keepdims=True) + p_own
            o = jnp.sum(p * vbuf[grp, hl], axis=0, keepdims=True) + p_own * vn_ref[0, h:h + 1, :]
            o_ref[0, h:h + 1, :] = (o / l).astype(o_ref.dtype)

    @pl.when(si == 0)
    def _():
        for grp in range(2):
            for cp in gather(si, grp):
                cp.start()

    for grp in range(2):
        for cp in gather(si, grp):
            cp.wait()
        attend(grp)

        @pl.when(si + 1 < ns)
        def _():
            for cp in gather(si + 1, grp):
                cp.start()


def _moba_step(q, k_new, v_new, cache_k, cache_v, page_table, sel):
    s, n_heads, hd = q.shape
    _, page, _, _ = cache_k.shape
    npages = page_table.shape[1]
    n_sel = sel.shape[1]
    npp = MOBA_BLOCK // page
    assert n_heads % 2 == 0
    keys = n_sel * MOBA_BLOCK
    row = pl.BlockSpec((1, n_heads, hd), lambda si, pt, sl: (si, 0, 0))
    grid_spec = pltpu.PrefetchScalarGridSpec(
        num_scalar_prefetch=2,
        grid=(s,),
        in_specs=[row, row, row, pl.BlockSpec(memory_space=pl.ANY),
                  pl.BlockSpec(memory_space=pl.ANY)],
        out_specs=row,
        scratch_shapes=[pltpu.VMEM((2, n_heads // 2, keys, hd), cache_k.dtype),
                        pltpu.VMEM((2, n_heads // 2, keys, hd), cache_v.dtype),
                        pltpu.SemaphoreType.DMA((2, 2))],
    )
    scratch_bytes = 2 * _nbytes((n_heads, keys, hd), cache_k.dtype)
    out = pl.pallas_call(
        functools.partial(_moba_step_kernel, n_sel=n_sel, npp=npp, npages=npages, scale=hd ** -0.5),
        grid_spec=grid_spec,
        out_shape=jax.ShapeDtypeStruct((s, n_heads, hd), F32),
        compiler_params=_params(("arbitrary",), 4 * _nbytes((n_heads, hd), F32), scratch_bytes),
        name="moba_step",
    )(page_table.reshape(-1), sel.reshape(-1), q, k_new, v_new, cache_k, cache_v)
    return out.reshape(s, n_heads * hd)


def _swiglu_block(hp, hs, norm_gain, w_gate, w_up, w_down, layer, tag):
    (xp,) = _rmsnorm(hp, [norm_gain], [BF16], f"ffn_norm_p{tag}")
    (xs,) = _rmsnorm(hs, [norm_gain], [BF16], f"ffn_norm_s{tag}")
    (ap,), (as_,) = _matmul(xp, xs, [(w_gate, layer), (w_up, layer)], _ep_swiglu, [BF16],
                            tm=512, tn=512, name=f"ffn_gate_up{tag}")
    wd = [(w_down, layer)]
    (pp,), (ps,) = _matmul(ap, as_, wd, _ep_identity, [F32], tm=512, tn=512, k_part=(0, 2),
                           name=f"ffn_down_a{tag}")
    (hp,), (hs,) = _matmul(ap, as_, wd, _ep_residual_partial, [F32], [pp, hp], [ps, hs],
                           tm=512, tn=512, k_part=(1, 2), name=f"ffn_down_b{tag}")
    return hp, hs


def kernel(x_prompt, x_sample, cache_k, cache_v, page_table, state_conv, state_h, mixer_norm, lru_w_x, lru_w_y, lru_conv_w, lru_conv_b, lru_w_a, lru_b_a, lru_w_i, lru_b_i, lru_lambda, lru_w_out, kv_norm, w_k, w_v, w_q, w_o, ffn_norm, w_gate, w_up, w_down, final_norm):
    bp, tp, d = x_prompt.shape
    bs, ts, _ = x_sample.shape
    n_heads, hd = cache_k.shape[2], cache_k.shape[3]
    page = cache_k.shape[1]
    past_len = page_table.shape[1] * page
    depth = mixer_norm.shape[0]
    n_a = lru_w_x.shape[0]
    assert ts == 1, "the decode group advances one position per step"
    assert MOBA_BLOCK % page == 0 and past_len % MOBA_BLOCK == 0
    n_sel_s = min(MOBA_TOP_K, past_len // MOBA_BLOCK)
    assert past_len // MOBA_BLOCK >= n_sel_s

    hp = x_prompt.reshape(bp * tp, d)
    hs = x_sample.reshape(bs * ts, d)
    conv_p, conv_s, hl_p, hl_s = [], [], [], []
    k_p = v_p = k_s = v_s = None

    for layer in range(depth):
        (xp,) = _rmsnorm(hp, [mixer_norm[layer]], [BF16], f"mixer_norm_p{layer}")
        (xs,) = _rmsnorm(hs, [mixer_norm[layer]], [BF16], f"mixer_norm_s{layer}")
        if layer < n_a:
            a = layer
            lru = (lru_conv_w[a], lru_conv_b[a], lru_w_a[a], lru_b_a[a], lru_w_i[a], lru_b_i[a],
                   lru_lambda[a])
            cwid = lru_conv_w.shape[1]
            (xr_p, gate_p), (xr_s, gate_s) = _matmul(
                xp, xs, [(lru_w_x, a), (lru_w_y, a)], _ep_x_gelu, [F32, F32], tm=512, tn=512,
                name=f"lru_in{layer}")
            w = xr_p.shape[1]
            xr3 = xr_p.reshape(bp, tp, w)
            hg_p, hlast_p = _rglru_seq(xr3, gate_p.reshape(bp, tp, w),
                                       jnp.zeros((bp, cwid - 1, w), F32), jnp.zeros((bp, w), F32),
                                       *lru, first_pos_zero=True)
            hg_s, hlast_s, nbuf = _rglru_step(xr_s, gate_s, state_conv[a], state_h[a], *lru,
                                              first_pos_zero=(past_len == 0))
            (hp,), (hs,) = _matmul(hg_p.reshape(bp * tp, w), hg_s, [(lru_w_out, a)], _ep_residual,
                                   [F32], [hp], [hs], tm=256, tn=1024, name=f"lru_out{layer}")
            conv_p.append(xr3[:, tp - (cwid - 1):, :])
            conv_s.append(nbuf)
            hl_p.append(hlast_p)
            hl_s.append(hlast_s)
        else:
            b = layer - n_a
            (q_p,), (q_s,) = _matmul(xp, xs, [(w_q, b)], _ep_identity, [F32], tm=256, tn=1024,
                                     name=f"attn_q{layer}")
            o_p, kmean_s = _moba_seq(q_p.reshape(bp, tp, d), k_p.reshape(bp, tp, d),
                                     v_p.reshape(bp, tp, d), n_heads, cache_k, page_table)
            q3 = q_s.reshape(bs, n_heads, hd)
            sel = _block_select(q3, kmean_s, n_sel_s)
            o_s = _moba_step(q3, k_s.reshape(bs, n_heads, hd), v_s.reshape(bs, n_heads, hd),
                             cache_k, cache_v, page_table, sel)
            (hp,), (hs,) = _matmul(o_p.reshape(bp * tp, d), o_s.astype(BF16), [(w_o, b)],
                                   _ep_residual, [F32], [hp], [hs], tm=256, tn=1024,
                                   name=f"attn_o{layer}")

        hp, hs = _swiglu_block(hp, hs, ffn_norm[layer], w_gate, w_up, w_down, layer, str(layer))

        if layer == n_a - 1:
            (xp,) = _rmsnorm(hp, [kv_norm], [BF16], "kv_norm_p")
            (xs,) = _rmsnorm(hs, [kv_norm], [BF16], "kv_norm_s")
            (k_p, v_p), (k_s, v_s) = _matmul(
                xp, xs, [(w_k.reshape(1, *w_k.shape), 0), (w_v.reshape(1, *w_v.shape), 0)],
                _ep_identity, [F32, F32], tm=512, tn=512, name="kv")

    (y_p,) = _rmsnorm(hp, [final_norm], [F32], "final_norm_p")
    (y_s,) = _rmsnorm(hs, [final_norm], [F32], "final_norm_s")
    return (y_p.reshape(bp, tp, d), y_s.reshape(bs, ts, d),
            k_p.reshape(bp, tp, n_heads, hd), v_p.reshape(bp, tp, n_heads, hd),
            k_s.reshape(bs, ts, n_heads, hd), v_s.reshape(bs, ts, n_heads, hd),
            jnp.stack(conv_p), jnp.stack(conv_s), jnp.stack(hl_p), jnp.stack(hl_s))
```

```python
import functools

import jax
import jax.numpy as jnp
from jax import lax
from jax.experimental import pallas as pl
from jax.experimental.pallas import tpu as pltpu

F32 = jnp.float32
BF16 = jnp.bfloat16

RMS_EPS = 1e-6
LRU_C = 8.0
MOBA_BLOCK = 256
MOBA_TOP_K = 3
LOG2_E = 1.4426950408889634

V7X_VMEM_BYTES = 64 * 1024 * 1024
V7X_SUBLANES = 8
V7X_LANES = 128
_VMEM_INTERNAL_BYTES = 16 * 1024 * 1024


def _pick_tile(dim, target, align):
    best = None
    d = align
    while d <= min(dim, target):
        if dim % d == 0:
            best = d
        d += align
    return best if best is not None else dim


def _nbytes(shape, dtype):
    n = 1
    for s in shape:
        n *= s
    return n * jnp.dtype(dtype).itemsize


def _params(semantics, block_bytes, scratch_bytes=0):
    limit = 2 * block_bytes + scratch_bytes + _VMEM_INTERNAL_BYTES
    limit = min(max(limit, 16 * 1024 * 1024), V7X_VMEM_BYTES - 4 * 1024 * 1024)
    return pltpu.CompilerParams(dimension_semantics=semantics, vmem_limit_bytes=int(limit))


def _rmsnorm_kernel(x_ref, *refs, n_out):
    g_refs, o_refs = refs[:n_out], refs[n_out:]
    x = x_ref[...].astype(F32)
    y = x * lax.rsqrt(jnp.mean(x * x, axis=-1, keepdims=True) + RMS_EPS)
    for g_ref, o_ref in zip(g_refs, o_refs):
        o_ref[...] = (y * g_ref[...].astype(F32)).astype(o_ref.dtype)


def _rmsnorm(x, gains, out_dtypes, name):
    m, d = x.shape
    tm = _pick_tile(m, 256, V7X_SUBLANES)
    n_out = len(gains)
    block_bytes = _nbytes((tm, d), x.dtype) + sum(_nbytes((tm, d), dt) for dt in out_dtypes)
    outs = pl.pallas_call(
        functools.partial(_rmsnorm_kernel, n_out=n_out),
        grid=(m // tm,),
        in_specs=[pl.BlockSpec((tm, d), lambda i: (i, 0))]
        + [pl.BlockSpec((1, d), lambda i: (0, 0))] * n_out,
        out_specs=[pl.BlockSpec((tm, d), lambda i: (i, 0))] * n_out,
        out_shape=[jax.ShapeDtypeStruct((m, d), dt) for dt in out_dtypes],
        compiler_params=_params(("parallel",), block_bytes),
        name=name,
    )(x, *[g.reshape(1, d) for g in gains])
    return outs


def _matmul_kernel(*refs, n_w, n_extra, n_out, epilogue, nj, ck):
    ap_ref, as_ref = refs[0], refs[1]
    pos = 2
    w_refs = refs[pos:pos + n_w]
    pos += n_w
    ep_refs = refs[pos:pos + n_extra]
    pos += n_extra
    es_refs = refs[pos:pos + n_extra]
    pos += n_extra
    op_refs = refs[pos:pos + n_out]
    pos += n_out
    os_refs = refs[pos:pos + n_out]
    pos += n_out
    wb_refs = refs[pos:]
    s, i = pl.program_id(0), pl.program_id(1)
    staged = s % 2
    ready = (s + 1) % 2

    def stage():
        row = pl.multiple_of(i * ck, ck)
        for w_ref, wb_ref in zip(w_refs, wb_refs):
            wb_ref[staged, pl.ds(row, ck), :] = w_ref[...].astype(BF16)

    def apply(a_ref, e_refs, o_refs):
        a = a_ref[...]
        accs = [jnp.dot(a, wb_ref[ready], preferred_element_type=F32) for wb_ref in wb_refs]
        outs = epilogue(*accs, *[e[...] for e in e_refs])
        for o_ref, val in zip(o_refs, outs):
            o_ref[...] = val.astype(o_ref.dtype)

    def multiply():
        apply(ap_ref, ep_refs, op_refs)

        @pl.when(i == 0)
        def _():
            apply(as_ref, es_refs, os_refs)

    @pl.when(s == 0)
    def _():
        stage()

    @pl.when((s > 0) & (s < nj))
    def _():
        stage()
        multiply()

    @pl.when(s == nj)
    def _():
        multiply()


def _matmul(a_p, a_s, ws, epilogue, out_dtypes, extras_p=(), extras_s=(), *, tm, tn,
            k_part=(0, 1), name):
    m, kdim = a_p.shape
    s_rows = a_s.shape[0]
    n = ws[0][0].shape[2]
    kp, nkp = k_part
    assert kdim % nkp == 0
    tk = kdim // nkp
    tm = _pick_tile(m, tm, V7X_SUBLANES)
    ni = m // tm
    n_w, n_extra, n_out = len(ws), len(extras_p), len(out_dtypes)
    assert len(extras_s) == n_extra
    tn = min(tn, n)
    assert tn % V7X_LANES == 0
    nj = pl.cdiv(n, tn)
    assert tk % ni == 0 and (tk // ni) % (2 * V7X_SUBLANES) == 0
    ck = tk // ni

    def col(s):
        return jnp.maximum(s - 1, 0)

    def row(s, i):
        return jnp.where(s == 0, 0, i)

    def w_spec(layer):
        def index(s, i):
            return (layer, kp * ni + jnp.where(s == nj, ni - 1, i), jnp.minimum(s, nj - 1))
        return pl.BlockSpec((None, ck, tn), index)

    p_tile = pl.BlockSpec((tm, tn), lambda s, i: (row(s, i), col(s)))
    s_tile = pl.BlockSpec((s_rows, tn), lambda s, i: (0, col(s)))
    block_bytes = (_nbytes((tm + s_rows, tk), BF16) + n_w * _nbytes((ck, tn), F32)
                   + sum(_nbytes((tm + s_rows, tn), e.dtype) for e in extras_p)
                   + sum(_nbytes((tm + s_rows, tn), dt) for dt in out_dtypes))
    outs = pl.pallas_call(
        functools.partial(_matmul_kernel, n_w=n_w, n_extra=n_extra, n_out=n_out, epilogue=epilogue,
                          nj=nj, ck=ck),
        grid=(nj + 1, ni),
        in_specs=[pl.BlockSpec((tm, tk), lambda s, i: (row(s, i), kp)),
                  pl.BlockSpec((s_rows, tk), lambda s, i: (0, kp))]
        + [w_spec(layer) for _, layer in ws]
        + [p_tile] * n_extra + [s_tile] * n_extra,
        out_specs=[p_tile] * n_out + [s_tile] * n_out,
        out_shape=[jax.ShapeDtypeStruct((m, n), dt) for dt in out_dtypes]
        + [jax.ShapeDtypeStruct((s_rows, n), dt) for dt in out_dtypes],
        scratch_shapes=[pltpu.VMEM((2, tk, tn), BF16) for _ in ws],
        compiler_params=_params(("arbitrary", "arbitrary"), block_bytes,
                                2 * n_w * _nbytes((tk, tn), BF16)),
        name=name,
    )(a_p, a_s, *[w for w, _ in ws], *extras_p, *extras_s)
    return outs[:n_out], outs[n_out:]


def _ep_identity(*accs):
    return accs


def _ep_x_gelu(acc_x, acc_y):
    return acc_x, jax.nn.gelu(acc_y)


def _ep_residual(acc, res):
    return (res + acc,)


def _ep_residual_partial(acc, partial, res):
    return (res + (partial + acc),)


def _ep_swiglu(acc_gate, acc_up):
    return (jax.nn.silu(acc_gate) * acc_up,)


def _lru_gates(xc, head, wa_ref, ba_ref, wi_ref, bi_ref, lam):
    xb = xc.astype(BF16)
    r = jax.nn.sigmoid(jnp.dot(xb, wa_ref[head], preferred_element_type=F32) + ba_ref[head])
    i = jax.nn.sigmoid(jnp.dot(xb, wi_ref[head], preferred_element_type=F32) + bi_ref[head])
    log_a = LRU_C * r * jax.nn.log_sigmoid(lam)
    a = jnp.exp(log_a)
    mult = jnp.sqrt(-jnp.tanh(log_a) * (a * a + 1.0))
    return a, mult, i


def _rglru_seq_kernel(xr_ref, gate_ref, cbuf_ref, h0_ref, cw_ref, cb_ref, wa_ref, ba_ref, wi_ref,
                      bi_ref, lam_ref, hg_ref, hlast_ref, xp_scr, a_scr, u_scr, hs_scr, h_scr, *,
                      tt, nh, dh, cwid, first_pos_zero):
    t = pl.program_id(2)
    halo = cwid - 1
    base = V7X_SUBLANES - halo

    @pl.when(t == 0)
    def _():
        xp_scr[base:V7X_SUBLANES, :] = cbuf_ref[0].astype(F32)
        h_scr[...] = h0_ref[0].astype(F32)

    xp_scr[V7X_SUBLANES:V7X_SUBLANES + tt, :] = xr_ref[0]
    xc = cb_ref[...] + cw_ref[0:1, :] * xp_scr[base:base + tt, :]
    for j in range(1, cwid):
        xc = xc + cw_ref[j:j + 1, :] * xp_scr[base + j:base + j + tt, :]
    xp_scr[base:V7X_SUBLANES, :] = xp_scr[base + tt:V7X_SUBLANES + tt, :]

    for hd in range(nh):
        sl = slice(hd * dh, (hd + 1) * dh)
        xh = xc[:, sl]
        a, mult, i = _lru_gates(xh, hd, wa_ref, ba_ref, wi_ref, bi_ref, lam_ref[:, sl])
        if first_pos_zero:
            row = lax.broadcasted_iota(jnp.int32, mult.shape, 0)
            mult = jnp.where((row == 0) & (t == 0), 1.0, mult)
        a_scr[:, sl] = a
        u_scr[:, sl] = mult * i * xh

    def step(s, h):
        h = a_scr[pl.ds(s, 1), :] * h + u_scr[pl.ds(s, 1), :]
        hs_scr[pl.ds(s, 1), :] = h
        return h

    h = lax.fori_loop(0, tt, step, h_scr[...], unroll=8)
    h_scr[...] = h
    hg_ref[0] = (hs_scr[...] * gate_ref[0]).astype(hg_ref.dtype)
    hlast_ref[0] = h


def _rglru_seq(xr, gate, conv_buf, h0, conv_w, conv_b, w_a, b_a, w_i, b_i, lam, *, first_pos_zero):
    b, t, w = xr.shape
    nheads, dh = w_a.shape[0], w_a.shape[1]
    cwid = conv_w.shape[0]
    wc = _pick_tile(w, 1024, dh)
    nh = wc // dh
    tt = _pick_tile(t, 256, V7X_SUBLANES)
    blk = lambda bi_, ci, ti: (bi_, ti, ci)
    per_bc = lambda bi_, ci, ti: (bi_, 0, ci)
    chan = lambda bi_, ci, ti: (0, ci)
    head = lambda bi_, ci, ti: (ci, 0, 0)
    block_bytes = (2 * _nbytes((tt, wc), F32) + _nbytes((tt, wc), BF16)
                   + 2 * _nbytes((nh, dh, dh), BF16) + 16 * _nbytes((1, wc), F32))
    scratch_bytes = 4 * _nbytes((tt + V7X_SUBLANES, wc), F32)
    hg, hlast = pl.pallas_call(
        functools.partial(_rglru_seq_kernel, tt=tt, nh=nh, dh=dh, cwid=cwid,
                          first_pos_zero=first_pos_zero),
        grid=(b, w // wc, t // tt),
        in_specs=[
            pl.BlockSpec((1, tt, wc), blk),
            pl.BlockSpec((1, tt, wc), blk),
            pl.BlockSpec((1, cwid - 1, wc), per_bc),
            pl.BlockSpec((1, 1, wc), per_bc),
            pl.BlockSpec((cwid, wc), chan),
            pl.BlockSpec((1, wc), chan),
            pl.BlockSpec((nh, dh, dh), head),
            pl.BlockSpec((nh, 1, dh), head),
            pl.BlockSpec((nh, dh, dh), head),
            pl.BlockSpec((nh, 1, dh), head),
            pl.BlockSpec((1, wc), chan),
        ],
        out_specs=[pl.BlockSpec((1, tt, wc), blk), pl.BlockSpec((1, 1, wc), per_bc)],
        out_shape=[jax.ShapeDtypeStruct((b, t, w), BF16), jax.ShapeDtypeStruct((b, 1, w), F32)],
        scratch_shapes=[
            pltpu.VMEM((tt + V7X_SUBLANES, wc), F32),
            pltpu.VMEM((tt, wc), F32),
            pltpu.VMEM((tt, wc), F32),
            pltpu.VMEM((tt, wc), F32),
            pltpu.VMEM((1, wc), F32),
        ],
        compiler_params=_params(("parallel", "parallel", "arbitrary"), block_bytes, scratch_bytes),
        name="rglru_seq",
    )(xr, gate, conv_buf, h0.reshape(b, 1, w), conv_w, conv_b.reshape(1, w),
      w_a.astype(BF16), b_a.reshape(nheads, 1, dh), w_i.astype(BF16), b_i.reshape(nheads, 1, dh),
      lam.reshape(1, w))
    return hg, hlast.reshape(b, w)


def _rglru_step_kernel(xr_ref, gate_ref, cbuf_ref, h0_ref, cw_ref, cb_ref, wa_ref, ba_ref, wi_ref,
                       bi_ref, lam_ref, hg_ref, h_ref, nbuf_ref, *, nheads, dh, cwid,
                       first_pos_zero):
    xr = xr_ref[...]
    xc = cb_ref[...] + cw_ref[cwid - 1:cwid, :] * xr
    for j in range(cwid - 1):
        xc = xc + cw_ref[j:j + 1, :] * cbuf_ref[j]
    for j in range(cwid - 2):
        nbuf_ref[j] = cbuf_ref[j + 1]
    nbuf_ref[cwid - 2] = xr
    for hd in range(nheads):
        sl = slice(hd * dh, (hd + 1) * dh)
        xh = xc[:, sl]
        a, mult, i = _lru_gates(xh, hd, wa_ref, ba_ref, wi_ref, bi_ref, lam_ref[:, sl])
        if first_pos_zero:
            mult = jnp.ones_like(mult)
        h = a * h0_ref[:, sl] + mult * i * xh
        h_ref[:, sl] = h
        hg_ref[:, sl] = (h * gate_ref[:, sl]).astype(hg_ref.dtype)


def _rglru_step(xr, gate, conv_buf, h0, conv_w, conv_b, w_a, b_a, w_i, b_i, lam, *, first_pos_zero):
    s, w = xr.shape
    nheads, dh = w_a.shape[0], w_a.shape[1]
    cwid = conv_w.shape[0]
    hg, h, nbuf = pl.pallas_call(
        functools.partial(_rglru_step_kernel, nheads=nheads, dh=dh, cwid=cwid,
                          first_pos_zero=first_pos_zero),
        out_shape=[jax.ShapeDtypeStruct((s, w), BF16), jax.ShapeDtypeStruct((s, w), F32),
                   jax.ShapeDtypeStruct((cwid - 1, s, w), F32)],
        name="rglru_step",
    )(xr, gate, jnp.swapaxes(conv_buf, 0, 1), h0, conv_w, conv_b.reshape(1, w),
      w_a.astype(BF16), b_a.reshape(nheads, 1, dh), w_i.astype(BF16), b_i.reshape(nheads, 1, dh),
      lam.reshape(1, w))
    return hg, h, jnp.swapaxes(nbuf, 0, 1)


def _moba_seq_kernel(pt_ref, q_ref, k_ref, v_ref, *refs, nblk, blk, n_sel, scale, items, npp):
    page_refs, o_ref, km_ref = refs[:items * npp], refs[items * npp], refs[items * npp + 1]
    for it in range(items):
        acc = jnp.sum(page_refs[it * npp][0].astype(F32), axis=0)
        for p in range(1, npp):
            acc = acc + jnp.sum(page_refs[it * npp + p][0].astype(F32), axis=0)
        km_ref[it] = acc / blk

    k32 = k_ref[0]
    kb = k32.astype(BF16)
    v_t = v_ref[0].T.astype(BF16)
    nblk_pad = -(-nblk // V7X_SUBLANES) * V7X_SUBLANES
    means = [jnp.mean(k32[b * blk:(b + 1) * blk], axis=0, keepdims=True) for b in range(nblk)]
    means += [jnp.zeros_like(means[0])] * (nblk_pad - nblk)
    kmean = jnp.concatenate(means, axis=0)
    key_i = lax.broadcasted_iota(jnp.int32, (blk, blk), 0)
    qry_i = lax.broadcasted_iota(jnp.int32, (blk, blk), 1)
    causal = key_i <= qry_i
    neg_inf = jnp.float32(-jnp.inf)
    contract_last = (((1,), (1,)), ((), ()))
    exp2_scale = scale * LOG2_E

    for c in range(nblk):
        q32 = q_ref[0, c * blk:(c + 1) * blk, :]
        qb = q32.astype(BF16)
        sels = [None] * c
        if c > n_sel:
            gate = lax.dot_general(kmean, q32, contract_last, precision=lax.Precision.HIGHEST,
                                   preferred_element_type=F32)
            g = [gate[b:b + 1, :] for b in range(c)]
            for b in range(c):
                cnt = jnp.zeros((1, blk), jnp.int32)
                for b2 in range(c):
                    if b2 == b:
                        continue
                    beats = (g[b2] >= g[b]) if b2 < b else (g[b2] > g[b])
                    cnt = cnt + beats.astype(jnp.int32)
                sels[b] = cnt < n_sel
        s_blocks = []
        for b in range(c + 1):
            s = lax.dot_general(kb[b * blk:(b + 1) * blk], qb, contract_last,
                                preferred_element_type=F32)
            if b == c:
                s = jnp.where(causal, s, neg_inf)
            elif sels[b] is not None:
                s = jnp.where(sels[b], s, neg_inf)
            s_blocks.append(s)
        m = jnp.max(s_blocks[c], axis=0, keepdims=True)
        for b in range(c):
            m = jnp.maximum(m, jnp.max(s_blocks[b], axis=0, keepdims=True))
        l = jnp.zeros((1, blk), F32)
        o_t = jnp.zeros((v_t.shape[0], blk), F32)
        for b in range(c + 1):
            p = jnp.exp2((s_blocks[b] - m) * exp2_scale)
            l = l + jnp.sum(p, axis=0, keepdims=True)
            o_t = o_t + jnp.dot(v_t[:, b * blk:(b + 1) * blk], p.astype(BF16),
                                preferred_element_type=F32)
        o_ref[0, c * blk:(c + 1) * blk, :] = (o_t / l).T.astype(o_ref.dtype)


def _moba_seq(q, k, v, n_heads, cache_k, page_table):
    b, t, d = q.shape
    hd = d // n_heads
    assert t % MOBA_BLOCK == 0 and hd % V7X_LANES == 0
    nblk = t // MOBA_BLOCK
    _, page, c_heads, c_hd = cache_k.shape
    s, npages = page_table.shape
    npp = MOBA_BLOCK // page
    n_items = s * (npages // npp)
    steps = b * n_heads
    items = -(-n_items // steps)

    def page_spec(it, p):
        def index(bi, hi, pt):
            item = jnp.minimum((bi * n_heads + hi) * items + it, n_items - 1)
            return (pt[item * npp + p], 0, 0, 0)
        return pl.BlockSpec((1, page, c_heads, c_hd), index)

    spec = pl.BlockSpec((1, t, hd), lambda bi, hi, pt: (bi, 0, hi))
    grid_spec = pltpu.PrefetchScalarGridSpec(
        num_scalar_prefetch=1,
        grid=(b, n_heads),
        in_specs=[spec, spec, spec] + [page_spec(it, p) for it in range(items) for p in range(npp)],
        out_specs=[spec, pl.BlockSpec((items, c_heads, c_hd),
                                      lambda bi, hi, pt: (bi * n_heads + hi, 0, 0))],
    )
    block_bytes = (3 * _nbytes((t, hd), F32) + _nbytes((t, hd), BF16)
                   + items * npp * _nbytes((page, c_heads, c_hd), cache_k.dtype)
                   + items * _nbytes((c_heads, c_hd), F32))
    o, kmean = pl.pallas_call(
        functools.partial(_moba_seq_kernel, nblk=nblk, blk=MOBA_BLOCK,
                          n_sel=min(MOBA_TOP_K, nblk - 1), scale=hd ** -0.5, items=items, npp=npp),
        grid_spec=grid_spec,
        out_shape=[jax.ShapeDtypeStruct((b, t, d), BF16),
                   jax.ShapeDtypeStruct((steps * items, c_heads, c_hd), F32)],
        compiler_params=_params(("parallel", "parallel"), block_bytes),
        name="moba_seq",
    )(page_table.reshape(-1), q, k, v, *([cache_k] * (items * npp)))
    return o, kmean[:n_items].reshape(s, npages // npp, c_heads, c_hd)


def _block_select_kernel(q_ref, km_ref, sel_ref, *, n_sel):
    q = q_ref[0]
    km = km_ref[0]
    nb = km.shape[0]
    g = jnp.sum(km * q[None], axis=-1)
    blk_i = lax.broadcasted_iota(jnp.int32, g.shape, 0)
    rows = []
    for _ in range(n_sel):
        m = jnp.max(g, axis=0, keepdims=True)
        idx = jnp.min(jnp.where(g == m, blk_i, nb), axis=0, keepdims=True)
        rows.append(idx)
        g = jnp.where(blk_i == idx, -jnp.inf, g)
    sel_ref[0] = jnp.concatenate(rows, axis=0)


def _block_select(q, kmean, n_sel):
    s, nb, n_heads, hd = kmean.shape
    return pl.pallas_call(
        functools.partial(_block_select_kernel, n_sel=n_sel),
        grid=(s,),
        in_specs=[pl.BlockSpec((1, n_heads, hd), lambda si: (si, 0, 0)),
                  pl.BlockSpec((1, nb, n_heads, hd), lambda si: (si, 0, 0, 0))],
        out_specs=pl.BlockSpec((1, n_sel, n_heads), lambda si: (si, 0, 0)),
        out_shape=jax.ShapeDtypeStruct((s, n_sel, n_heads), jnp.int32),
        name="block_select",
    )(q, kmean)


def _moba_step_kernel(pt_ref, sel_ref, q_ref, kn_ref, vn_ref, ck_hbm, cv_hbm, o_ref, kbuf, vbuf,
                      sems, *, n_sel, npp, npages, scale):
    si, ns = pl.program_id(0), pl.num_programs(0)
    n_heads = q_ref.shape[1]
    page = ck_hbm.shape[1]
    group = n_heads // 2

    def gather(seq, grp):
        copies = []
        for hl in range(group):
            h = grp * group + hl
            for r in range(n_sel):
                blk = sel_ref[(seq * n_sel + r) * n_heads + h]
                for p in range(npp):
                    pg = pt_ref[seq * npages + blk * npp + p]
                    rows = pl.ds((r * npp + p) * page, page)
                    copies.append(pltpu.make_async_copy(
                        ck_hbm.at[pg, :, h, :], kbuf.at[grp, hl, rows, :], sems.at[0, grp]))
                    copies.append(pltpu.make_async_copy(
                        cv_hbm.at[pg, :, h, :], vbuf.at[grp, hl, rows, :], sems.at[1, grp]))
        return copies

    def attend(grp):
        for hl in range(group):
            h = grp * group + hl
            q = q_ref[0, h:h + 1, :]
            s = jnp.sum(kbuf[grp, hl] * q, axis=-1, keepdims=True) * scale
            s_own = jnp.sum(kn_ref[0, h:h + 1, :] * q, axis=-1, keepdims=True) * scale
            m = jnp.maximum(jnp.max(s, axis=0, keepdims=True), s_own)
            p = jnp.exp(s - m)
            p_own = jnp.exp(s_own - m)
            l = jnp.sum(p, axis=0, keepdims=True) + p_own
            o = jnp.sum(p * vbuf[grp, hl], axis=0, keepdims=True) + p_own * vn_ref[0, h:h + 1, :]
            o_ref[0, h:h + 1, :] = (o / l).astype(o_ref.dtype)

    @pl.when(si == 0)
    def _():
        for grp in range(2):
            for cp in gather(si, grp):
                cp.start()

    for grp in range(2):
        for cp in gather(si, grp):
            cp.wait()
        attend(grp)

        @pl.when(si + 1 < ns)
        def _():
            for cp in gather(si + 1, grp):
                cp.start()


def _moba_step(q, k_new, v_new, cache_k, cache_v, page_table, sel):
    s, n_heads, hd = q.shape
    _, page, _, _ = cache_k.shape
    npages = page_table.shape[1]
    n_sel = sel.shape[1]
    npp = MOBA_BLOCK // page
    assert n_heads % 2 == 0
    keys = n_sel * MOBA_BLOCK
    row = pl.BlockSpec((1, n_heads, hd), lambda si, pt, sl: (si, 0, 0))
    grid_spec = pltpu.PrefetchScalarGridSpec(
        num_scalar_prefetch=2,
        grid=(s,),
        in_specs=[row, row, row, pl.BlockSpec(memory_space=pl.ANY),
                  pl.BlockSpec(memory_space=pl.ANY)],
        out_specs=row,
        scratch_shapes=[pltpu.VMEM((2, n_heads // 2, keys, hd), cache_k.dtype),
                        pltpu.VMEM((2, n_heads // 2, keys, hd), cache_v.dtype),
                        pltpu.SemaphoreType.DMA((2, 2))],
    )
    scratch_bytes = 2 * _nbytes((n_heads, keys, hd), cache_k.dtype)
    out = pl.pallas_call(
        functools.partial(_moba_step_kernel, n_sel=n_sel, npp=npp, npages=npages, scale=hd ** -0.5),
        grid_spec=grid_spec,
        out_shape=jax.ShapeDtypeStruct((s, n_heads, hd), F32),
        compiler_params=_params(("arbitrary",), 4 * _nbytes((n_heads, hd), F32), scratch_bytes),
        name="moba_step",
    )(page_table.reshape(-1), sel.reshape(-1), q, k_new, v_new, cache_k, cache_v)
    return out.reshape(s, n_heads * hd)


def _swiglu_block(hp, hs, norm_gain, w_gate, w_up, w_down, layer, tag):
    (xp,) = _rmsnorm(hp, [norm_gain], [BF16], f"ffn_norm_p{tag}")
    (xs,) = _rmsnorm(hs, [norm_gain], [BF16], f"ffn_norm_s{tag}")
    (ap,), (as_,) = _matmul(xp, xs, [(w_gate, layer), (w_up, layer)], _ep_swiglu, [BF16],
                            tm=1024, tn=512, name=f"ffn_gate_up{tag}")
    wd = [(w_down, layer)]
    (pp,), (ps,) = _matmul(ap, as_, wd, _ep_identity, [F32], tm=1024, tn=512, k_part=(0, 2),
                           name=f"ffn_down_a{tag}")
    (hp,), (hs,) = _matmul(ap, as_, wd, _ep_residual_partial, [F32], [pp, hp], [ps, hs],
                           tm=1024, tn=512, k_part=(1, 2), name=f"ffn_down_b{tag}")
    return hp, hs


def kernel(x_prompt, x_sample, cache_k, cache_v, page_table, state_conv, state_h, mixer_norm, lru_w_x, lru_w_y, lru_conv_w, lru_conv_b, lru_w_a, lru_b_a, lru_w_i, lru_b_i, lru_lambda, lru_w_out, kv_norm, w_k, w_v, w_q, w_o, ffn_norm, w_gate, w_up, w_down, final_norm):
    bp, tp, d = x_prompt.shape
    bs, ts, _ = x_sample.shape
    n_heads, hd = cache_k.shape[2], cache_k.shape[3]
    page = cache_k.shape[1]
    past_len = page_table.shape[1] * page
    depth = mixer_norm.shape[0]
    n_a = lru_w_x.shape[0]
    assert ts == 1, "the decode group advances one position per step"
    assert MOBA_BLOCK % page == 0 and past_len % MOBA_BLOCK == 0
    n_sel_s = min(MOBA_TOP_K, past_len // MOBA_BLOCK)
    assert past_len // MOBA_BLOCK >= n_sel_s

    hp = x_prompt.reshape(bp * tp, d)
    hs = x_sample.reshape(bs * ts, d)
    conv_p, conv_s, hl_p, hl_s = [], [], [], []
    k_p = v_p = k_s = v_s = None

    for layer in range(depth):
        (xp,) = _rmsnorm(hp, [mixer_norm[layer]], [BF16], f"mixer_norm_p{layer}")
        (xs,) = _rmsnorm(hs, [mixer_norm[layer]], [BF16], f"mixer_norm_s{layer}")
        if layer < n_a:
            a = layer
            lru = (lru_conv_w[a], lru_conv_b[a], lru_w_a[a], lru_b_a[a], lru_w_i[a], lru_b_i[a],
                   lru_lambda[a])
            cwid = lru_conv_w.shape[1]
            (xr_p, gate_p), (xr_s, gate_s) = _matmul(
                xp, xs, [(lru_w_x, a), (lru_w_y, a)], _ep_x_gelu, [F32, F32], tm=1024, tn=512,
                name=f"lru_in{layer}")
            w = xr_p.shape[1]
            xr3 = xr_p.reshape(bp, tp, w)
            hg_p, hlast_p = _rglru_seq(xr3, gate_p.reshape(bp, tp, w),
                                       jnp.zeros((bp, cwid - 1, w), F32), jnp.zeros((bp, w), F32),
                                       *lru, first_pos_zero=True)
            hg_s, hlast_s, nbuf = _rglru_step(xr_s, gate_s, state_conv[a], state_h[a], *lru,
                                              first_pos_zero=(past_len == 0))
            (hp,), (hs,) = _matmul(hg_p.reshape(bp * tp, w), hg_s, [(lru_w_out, a)], _ep_residual,
                                   [F32], [hp], [hs], tm=1024, tn=512, name=f"lru_out{layer}")
            conv_p.append(xr3[:, tp - (cwid - 1):, :])
            conv_s.append(nbuf)
            hl_p.append(hlast_p)
            hl_s.append(hlast_s)
        else:
            b = layer - n_a
            (q_p,), (q_s,) = _matmul(xp, xs, [(w_q, b)], _ep_identity, [F32], tm=1024, tn=512,
                                     name=f"attn_q{layer}")
            o_p, kmean_s = _moba_seq(q_p.reshape(bp, tp, d), k_p.reshape(bp, tp, d),
                                     v_p.reshape(bp, tp, d), n_heads, cache_k, page_table)
            q3 = q_s.reshape(bs, n_heads, hd)
            sel = _block_select(q3, kmean_s, n_sel_s)
            o_s = _moba_step(q3, k_s.reshape(bs, n_heads, hd), v_s.reshape(bs, n_heads, hd),
                             cache_k, cache_v, page_table, sel)
            (hp,), (hs,) = _matmul(o_p.reshape(bp * tp, d), o_s.astype(BF16), [(w_o, b)],
                                   _ep_residual, [F32], [hp], [hs], tm=1024, tn=512,
                                   name=f"attn_o{layer}")

        hp, hs = _swiglu_block(hp, hs, ffn_norm[layer], w_gate, w_up, w_down, layer, str(layer))

        if layer == n_a - 1:
            (xp,) = _rmsnorm(hp, [kv_norm], [BF16], "kv_norm_p")
            (xs,) = _rmsnorm(hs, [kv_norm], [BF16], "kv_norm_s")
            (k_p, v_p), (k_s, v_s) = _matmul(
                xp, xs, [(w_k.reshape(1, *w_k.shape), 0), (w_v.reshape(1, *w_v.shape), 0)],
                _ep_identity, [F32, F32], tm=1024, tn=512, name="kv")

    (y_p,) = _rmsnorm(hp, [final_norm], [F32], "final_norm_p")
    (y_s,) = _rmsnorm(hs, [final_norm], [F32], "final_norm_s")
    return (y_p.reshape(bp, tp, d), y_s.reshape(bs, ts, d),
            k_p.reshape(bp, tp, n_heads, hd), v_p.reshape(bp, tp, n_heads, hd),
            k_s.reshape(bs, ts, n_heads, hd), v_s.reshape(bs, ts, n_heads, hd),
            jnp.stack(conv_p), jnp.stack(conv_s), jnp.stack(hl_p), jnp.stack(hl_s))
```

```python
import functools

import jax
import jax.numpy as jnp
from jax import lax
from jax.experimental import pallas as pl
from jax.experimental.pallas import tpu as pltpu

F32 = jnp.float32
BF16 = jnp.bfloat16

RMS_EPS = 1e-6
LRU_C = 8.0
MOBA_BLOCK = 256
MOBA_TOP_K = 3
LOG2_E = 1.4426950408889634

V7X_VMEM_BYTES = 64 * 1024 * 1024
V7X_SUBLANES = 8
V7X_LANES = 128
_VMEM_INTERNAL_BYTES = 16 * 1024 * 1024


def _pick_tile(dim, target, align):
    best = None
    d = align
    while d <= min(dim, target):
        if dim % d == 0:
            best = d
        d += align
    return best if best is not None else dim


def _nbytes(shape, dtype):
    n = 1
    for s in shape:
        n *= s
    return n * jnp.dtype(dtype).itemsize


def _params(semantics, block_bytes, scratch_bytes=0):
    limit = 2 * block_bytes + scratch_bytes + _VMEM_INTERNAL_BYTES
    limit = min(max(limit, 16 * 1024 * 1024), V7X_VMEM_BYTES - 4 * 1024 * 1024)
    return pltpu.CompilerParams(dimension_semantics=semantics, vmem_limit_bytes=int(limit))


def _rmsnorm_kernel(x_ref, *refs, n_out):
    g_refs, o_refs = refs[:n_out], refs[n_out:]
    x = x_ref[...].astype(F32)
    y = x * lax.rsqrt(jnp.mean(x * x, axis=-1, keepdims=True) + RMS_EPS)
    for g_ref, o_ref in zip(g_refs, o_refs):
        o_ref[...] = (y * g_ref[...].astype(F32)).astype(o_ref.dtype)


def _rmsnorm(x, gains, out_dtypes, name):
    m, d = x.shape
    tm = _pick_tile(m, 256, V7X_SUBLANES)
    n_out = len(gains)
    block_bytes = _nbytes((tm, d), x.dtype) + sum(_nbytes((tm, d), dt) for dt in out_dtypes)
    outs = pl.pallas_call(
        functools.partial(_rmsnorm_kernel, n_out=n_out),
        grid=(m // tm,),
        in_specs=[pl.BlockSpec((tm, d), lambda i: (i, 0))]
        + [pl.BlockSpec((1, d), lambda i: (0, 0))] * n_out,
        out_specs=[pl.BlockSpec((tm, d), lambda i: (i, 0))] * n_out,
        out_shape=[jax.ShapeDtypeStruct((m, d), dt) for dt in out_dtypes],
        compiler_params=_params(("parallel",), block_bytes),
        name=name,
    )(x, *[g.reshape(1, d) for g in gains])
    return outs


def _matmul_kernel(*refs, n_w, n_extra, n_out, epilogue, nj, ck):
    ap_ref, as_ref = refs[0], refs[1]
    pos = 2
    w_refs = refs[pos:pos + n_w]
    pos += n_w
    ep_refs = refs[pos:pos + n_extra]
    pos += n_extra
    es_refs = refs[pos:pos + n_extra]
    pos += n_extra
    op_refs = refs[pos:pos + n_out]
    pos += n_out
    os_refs = refs[pos:pos + n_out]
    pos += n_out
    wb_refs = refs[pos:]
    s, i = pl.program_id(0), pl.program_id(1)
    staged = s % 2
    ready = (s + 1) % 2

    def stage():
        row = pl.multiple_of(i * ck, ck)
        for w_ref, wb_ref in zip(w_refs, wb_refs):
            wb_ref[staged, pl.ds(row, ck), :] = w_ref[...].astype(BF16)

    def apply(a_ref, e_refs, o_refs):
        a = a_ref[...]
        accs = [jnp.dot(a, wb_ref[ready], preferred_element_type=F32) for wb_ref in wb_refs]
        outs = epilogue(*accs, *[e[...] for e in e_refs])
        for o_ref, val in zip(o_refs, outs):
            o_ref[...] = val.astype(o_ref.dtype)

    def multiply():
        apply(ap_ref, ep_refs, op_refs)

        @pl.when(i == 0)
        def _():
            apply(as_ref, es_refs, os_refs)

    @pl.when(s == 0)
    def _():
        stage()

    @pl.when((s > 0) & (s < nj))
    def _():
        stage()
        multiply()

    @pl.when(s == nj)
    def _():
        multiply()


def _matmul(a_p, a_s, ws, epilogue, out_dtypes, extras_p=(), extras_s=(), *, tm, tn,
            k_part=(0, 1), name):
    m, kdim = a_p.shape
    s_rows = a_s.shape[0]
    n = ws[0][0].shape[2]
    kp, nkp = k_part
    assert kdim % nkp == 0
    tk = kdim // nkp
    tm = _pick_tile(m, tm, V7X_SUBLANES)
    ni = m // tm
    n_w, n_extra, n_out = len(ws), len(extras_p), len(out_dtypes)
    assert len(extras_s) == n_extra
    tn = min(tn, n)
    assert tn % V7X_LANES == 0
    nj = pl.cdiv(n, tn)
    assert tk % ni == 0 and (tk // ni) % (2 * V7X_SUBLANES) == 0
    ck = tk // ni

    def col(s):
        return jnp.maximum(s - 1, 0)

    def row(s, i):
        return jnp.where(s == 0, 0, i)

    def w_spec(layer):
        def index(s, i):
            return (layer, kp * ni + jnp.where(s == nj, ni - 1, i), jnp.minimum(s, nj - 1))
        return pl.BlockSpec((None, ck, tn), index)

    p_tile = pl.BlockSpec((tm, tn), lambda s, i: (row(s, i), col(s)))
    s_tile = pl.BlockSpec((s_rows, tn), lambda s, i: (0, col(s)))
    block_bytes = (_nbytes((tm + s_rows, tk), BF16) + n_w * _nbytes((ck, tn), F32)
                   + sum(_nbytes((tm + s_rows, tn), e.dtype) for e in extras_p)
                   + sum(_nbytes((tm + s_rows, tn), dt) for dt in out_dtypes))
    outs = pl.pallas_call(
        functools.partial(_matmul_kernel, n_w=n_w, n_extra=n_extra, n_out=n_out, epilogue=epilogue,
                          nj=nj, ck=ck),
        grid=(nj + 1, ni),
        in_specs=[pl.BlockSpec((tm, tk), lambda s, i: (row(s, i), kp)),
                  pl.BlockSpec((s_rows, tk), lambda s, i: (0, kp))]
        + [w_spec(layer) for _, layer in ws]
        + [p_tile] * n_extra + [s_tile] * n_extra,
        out_specs=[p_tile] * n_out + [s_tile] * n_out,
        out_shape=[jax.ShapeDtypeStruct((m, n), dt) for dt in out_dtypes]
        + [jax.ShapeDtypeStruct((s_rows, n), dt) for dt in out_dtypes],
        scratch_shapes=[pltpu.VMEM((2, tk, tn), BF16) for _ in ws],
        compiler_params=_params(("arbitrary", "arbitrary"), block_bytes,
                                2 * n_w * _nbytes((tk, tn), BF16)),
        name=name,
    )(a_p, a_s, *[w for w, _ in ws], *extras_p, *extras_s)
    return outs[:n_out], outs[n_out:]


def _ep_identity(*accs):
    return accs


def _ep_x_gelu(acc_x, acc_y):
    return acc_x, jax.nn.gelu(acc_y)


def _ep_residual(acc, res):
    return (res + acc,)


def _ep_residual_partial(acc, partial, res):
    return (res + (partial + acc),)


def _ep_swiglu(acc_gate, acc_up):
    return (jax.nn.silu(acc_gate) * acc_up,)


def _cache_mean_specs(cache_k, page_table, first, count, steps, step_of):
    _, page, c_heads, c_hd = cache_k.shape
    npp = MOBA_BLOCK // page
    per = -(-count // steps)

    def page_spec(it, p):
        def index(*args):
            *grid_idx, pt = args
            item = first + jnp.minimum(step_of(*grid_idx) * per + it, count - 1)
            return (pt[item * npp + p], 0, 0, 0)
        return pl.BlockSpec((1, page, c_heads, c_hd), index)

    specs = [page_spec(it, p) for it in range(per) for p in range(npp)]
    out_spec = pl.BlockSpec((per, c_heads, c_hd), lambda *args: (step_of(*args[:-1]), 0, 0))
    out_shape = jax.ShapeDtypeStruct((steps * per, c_heads, c_hd), F32)
    return per, specs, out_spec, out_shape


def _reduce_cache_blocks(page_refs, km_ref, per, npp):
    for it in range(per):
        acc = jnp.sum(page_refs[it * npp][0].astype(F32), axis=0)
        for p in range(1, npp):
            acc = acc + jnp.sum(page_refs[it * npp + p][0].astype(F32), axis=0)
        km_ref[it] = acc / MOBA_BLOCK


def _lru_gates(xc, head, wa_ref, ba_ref, wi_ref, bi_ref, lam):
    xb = xc.astype(BF16)
    r = jax.nn.sigmoid(jnp.dot(xb, wa_ref[head], preferred_element_type=F32) + ba_ref[head])
    i = jax.nn.sigmoid(jnp.dot(xb, wi_ref[head], preferred_element_type=F32) + bi_ref[head])
    log_a = LRU_C * r * jax.nn.log_sigmoid(lam)
    a = jnp.exp(log_a)
    mult = jnp.sqrt(-jnp.tanh(log_a) * (a * a + 1.0))
    return a, mult, i


def _rglru_seq_kernel(pt_ref, xr_ref, gate_ref, cbuf_ref, h0_ref, cw_ref, cb_ref, wa_ref, ba_ref,
                      wi_ref, bi_ref, lam_ref, *refs, tt, nh, dh, cwid, first_pos_zero, per, npp):
    n_pg = per * npp
    page_refs = refs[:n_pg]
    hg_ref, hlast_ref, km_ref, xp_scr, a_scr, u_scr, hs_scr, h_scr = refs[n_pg:]
    _reduce_cache_blocks(page_refs, km_ref, per, npp)
    t = pl.program_id(2)
    halo = cwid - 1
    base = V7X_SUBLANES - halo

    @pl.when(t == 0)
    def _():
        xp_scr[base:V7X_SUBLANES, :] = cbuf_ref[0].astype(F32)
        h_scr[...] = h0_ref[0].astype(F32)

    xp_scr[V7X_SUBLANES:V7X_SUBLANES + tt, :] = xr_ref[0]
    xc = cb_ref[...] + cw_ref[0:1, :] * xp_scr[base:base + tt, :]
    for j in range(1, cwid):
        xc = xc + cw_ref[j:j + 1, :] * xp_scr[base + j:base + j + tt, :]
    xp_scr[base:V7X_SUBLANES, :] = xp_scr[base + tt:V7X_SUBLANES + tt, :]

    for hd in range(nh):
        sl = slice(hd * dh, (hd + 1) * dh)
        xh = xc[:, sl]
        a, mult, i = _lru_gates(xh, hd, wa_ref, ba_ref, wi_ref, bi_ref, lam_ref[:, sl])
        if first_pos_zero:
            row = lax.broadcasted_iota(jnp.int32, mult.shape, 0)
            mult = jnp.where((row == 0) & (t == 0), 1.0, mult)
        a_scr[:, sl] = a
        u_scr[:, sl] = mult * i * xh

    def step(s, h):
        h = a_scr[pl.ds(s, 1), :] * h + u_scr[pl.ds(s, 1), :]
        hs_scr[pl.ds(s, 1), :] = h
        return h

    h = lax.fori_loop(0, tt, step, h_scr[...], unroll=8)
    h_scr[...] = h
    hg_ref[0] = (hs_scr[...] * gate_ref[0]).astype(hg_ref.dtype)
    hlast_ref[0] = h


def _rglru_seq(xr, gate, conv_buf, h0, conv_w, conv_b, w_a, b_a, w_i, b_i, lam, cache_k, page_table,
               cache_blocks, *, first_pos_zero):
    b, t, w = xr.shape
    nheads, dh = w_a.shape[0], w_a.shape[1]
    cwid = conv_w.shape[0]
    wc = _pick_tile(w, 1024, dh)
    nh = wc // dh
    tt = _pick_tile(t, 256, V7X_SUBLANES)
    n_c, n_t = w // wc, t // tt
    blk = lambda bi_, ci, ti, *_: (bi_, ti, ci)
    per_bc = lambda bi_, ci, ti, *_: (bi_, 0, ci)
    chan = lambda bi_, ci, ti, *_: (0, ci)
    head = lambda bi_, ci, ti, *_: (ci, 0, 0)
    npp = MOBA_BLOCK // cache_k.shape[1]
    per, page_specs, km_spec, km_shape = _cache_mean_specs(
        cache_k, page_table, 0, cache_blocks, b * n_c * n_t,
        lambda bi_, ci, ti: (bi_ * n_c + ci) * n_t + ti)
    block_bytes = (2 * _nbytes((tt, wc), F32) + _nbytes((tt, wc), BF16)
                   + 2 * _nbytes((nh, dh, dh), BF16) + 16 * _nbytes((1, wc), F32)
                   + per * npp * _nbytes(cache_k.shape[1:], cache_k.dtype))
    scratch_bytes = 4 * _nbytes((tt + V7X_SUBLANES, wc), F32)
    hg, hlast, kmean = pl.pallas_call(
        functools.partial(_rglru_seq_kernel, tt=tt, nh=nh, dh=dh, cwid=cwid,
                          first_pos_zero=first_pos_zero, per=per, npp=npp),
        grid_spec=pltpu.PrefetchScalarGridSpec(
            num_scalar_prefetch=1,
            grid=(b, n_c, n_t),
            in_specs=[
                pl.BlockSpec((1, tt, wc), blk),
                pl.BlockSpec((1, tt, wc), blk),
                pl.BlockSpec((1, cwid - 1, wc), per_bc),
                pl.BlockSpec((1, 1, wc), per_bc),
                pl.BlockSpec((cwid, wc), chan),
                pl.BlockSpec((1, wc), chan),
                pl.BlockSpec((nh, dh, dh), head),
                pl.BlockSpec((nh, 1, dh), head),
                pl.BlockSpec((nh, dh, dh), head),
                pl.BlockSpec((nh, 1, dh), head),
                pl.BlockSpec((1, wc), chan),
            ] + page_specs,
            out_specs=[pl.BlockSpec((1, tt, wc), blk), pl.BlockSpec((1, 1, wc), per_bc), km_spec],
            scratch_shapes=[
                pltpu.VMEM((tt + V7X_SUBLANES, wc), F32),
                pltpu.VMEM((tt, wc), F32),
                pltpu.VMEM((tt, wc), F32),
                pltpu.VMEM((tt, wc), F32),
                pltpu.VMEM((1, wc), F32),
            ]),
        out_shape=[jax.ShapeDtypeStruct((b, t, w), BF16), jax.ShapeDtypeStruct((b, 1, w), F32),
                   km_shape],
        compiler_params=_params(("parallel", "parallel", "arbitrary"), block_bytes, scratch_bytes),
        name="rglru_seq",
    )(page_table.reshape(-1), xr, gate, conv_buf, h0.reshape(b, 1, w), conv_w, conv_b.reshape(1, w),
      w_a.astype(BF16), b_a.reshape(nheads, 1, dh), w_i.astype(BF16), b_i.reshape(nheads, 1, dh),
      lam.reshape(1, w), *([cache_k] * (per * npp)))
    return hg, hlast.reshape(b, w), kmean[:cache_blocks]


def _rglru_step_kernel(xr_ref, gate_ref, cbuf_ref, h0_ref, cw_ref, cb_ref, wa_ref, ba_ref, wi_ref,
                       bi_ref, lam_ref, hg_ref, h_ref, nbuf_ref, *, nheads, dh, cwid,
                       first_pos_zero):
    xr = xr_ref[...]
    xc = cb_ref[...] + cw_ref[cwid - 1:cwid, :] * xr
    for j in range(cwid - 1):
        xc = xc + cw_ref[j:j + 1, :] * cbuf_ref[j]
    for j in range(cwid - 2):
        nbuf_ref[j] = cbuf_ref[j + 1]
    nbuf_ref[cwid - 2] = xr
    for hd in range(nheads):
        sl = slice(hd * dh, (hd + 1) * dh)
        xh = xc[:, sl]
        a, mult, i = _lru_gates(xh, hd, wa_ref, ba_ref, wi_ref, bi_ref, lam_ref[:, sl])
        if first_pos_zero:
            mult = jnp.ones_like(mult)
        h = a * h0_ref[:, sl] + mult * i * xh
        h_ref[:, sl] = h
        hg_ref[:, sl] = (h * gate_ref[:, sl]).astype(hg_ref.dtype)


def _rglru_step(xr, gate, conv_buf, h0, conv_w, conv_b, w_a, b_a, w_i, b_i, lam, *, first_pos_zero):
    s, w = xr.shape
    nheads, dh = w_a.shape[0], w_a.shape[1]
    cwid = conv_w.shape[0]
    hg, h, nbuf = pl.pallas_call(
        functools.partial(_rglru_step_kernel, nheads=nheads, dh=dh, cwid=cwid,
                          first_pos_zero=first_pos_zero),
        out_shape=[jax.ShapeDtypeStruct((s, w), BF16), jax.ShapeDtypeStruct((s, w), F32),
                   jax.ShapeDtypeStruct((cwid - 1, s, w), F32)],
        name="rglru_step",
    )(xr, gate, jnp.swapaxes(conv_buf, 0, 1), h0, conv_w, conv_b.reshape(1, w),
      w_a.astype(BF16), b_a.reshape(nheads, 1, dh), w_i.astype(BF16), b_i.reshape(nheads, 1, dh),
      lam.reshape(1, w))
    return hg, h, jnp.swapaxes(nbuf, 0, 1)


def _moba_seq_kernel(pt_ref, q_ref, k_ref, v_ref, *refs, nblk, blk, n_sel, scale, per, npp):
    page_refs, o_ref, km_ref = refs[:per * npp], refs[per * npp], refs[per * npp + 1]
    _reduce_cache_blocks(page_refs, km_ref, per, npp)

    k32 = k_ref[0]
    kb = k32.astype(BF16)
    v_t = v_ref[0].T.astype(BF16)
    nblk_pad = -(-nblk // V7X_SUBLANES) * V7X_SUBLANES
    means = [jnp.mean(k32[b * blk:(b + 1) * blk], axis=0, keepdims=True) for b in range(nblk)]
    means += [jnp.zeros_like(means[0])] * (nblk_pad - nblk)
    kmean = jnp.concatenate(means, axis=0)
    key_i = lax.broadcasted_iota(jnp.int32, (blk, blk), 0)
    qry_i = lax.broadcasted_iota(jnp.int32, (blk, blk), 1)
    causal = key_i <= qry_i
    neg_inf = jnp.float32(-jnp.inf)
    contract_last = (((1,), (1,)), ((), ()))
    exp2_scale = scale * LOG2_E

    for c in range(nblk):
        q32 = q_ref[0, c * blk:(c + 1) * blk, :]
        qb = q32.astype(BF16)
        sels = [None] * c
        if c > n_sel:
            gate = lax.dot_general(kmean, q32, contract_last, precision=lax.Precision.HIGHEST,
                                   preferred_element_type=F32)
            g = [gate[b:b + 1, :] for b in range(c)]
            for b in range(c):
                cnt = jnp.zeros((1, blk), jnp.int32)
                for b2 in range(c):
                    if b2 == b:
                        continue
                    beats = (g[b2] >= g[b]) if b2 < b else (g[b2] > g[b])
                    cnt = cnt + beats.astype(jnp.int32)
                sels[b] = cnt < n_sel
        s_blocks = []
        for b in range(c + 1):
            s = lax.dot_general(kb[b * blk:(b + 1) * blk], qb, contract_last,
                                preferred_element_type=F32)
            if b == c:
                s = jnp.where(causal, s, neg_inf)
            elif sels[b] is not None:
                s = jnp.where(sels[b], s, neg_inf)
            s_blocks.append(s)
        m = jnp.max(s_blocks[c], axis=0, keepdims=True)
        for b in range(c):
            m = jnp.maximum(m, jnp.max(s_blocks[b], axis=0, keepdims=True))
        l = jnp.zeros((1, blk), F32)
        o_t = jnp.zeros((v_t.shape[0], blk), F32)
        for b in range(c + 1):
            p = jnp.exp2((s_blocks[b] - m) * exp2_scale)
            l = l + jnp.sum(p, axis=0, keepdims=True)
            o_t = o_t + jnp.dot(v_t[:, b * blk:(b + 1) * blk], p.astype(BF16),
                                preferred_element_type=F32)
        o_ref[0, c * blk:(c + 1) * blk, :] = (o_t / l).T.astype(o_ref.dtype)


def _moba_seq(q, k, v, n_heads, cache_k, page_table, first_block, cache_blocks):
    b, t, d = q.shape
    hd = d // n_heads
    assert t % MOBA_BLOCK == 0 and hd % V7X_LANES == 0
    nblk = t // MOBA_BLOCK
    npp = MOBA_BLOCK // cache_k.shape[1]
    per, page_specs, km_spec, km_shape = _cache_mean_specs(
        cache_k, page_table, first_block, cache_blocks, b * n_heads,
        lambda bi, hi: bi * n_heads + hi)
    spec = pl.BlockSpec((1, t, hd), lambda bi, hi, pt: (bi, 0, hi))
    grid_spec = pltpu.PrefetchScalarGridSpec(
        num_scalar_prefetch=1,
        grid=(b, n_heads),
        in_specs=[spec, spec, spec] + page_specs,
        out_specs=[spec, km_spec],
    )
    block_bytes = (3 * _nbytes((t, hd), F32) + _nbytes((t, hd), BF16)
                   + per * npp * _nbytes(cache_k.shape[1:], cache_k.dtype)
                   + per * _nbytes(cache_k.shape[2:], F32))
    o, kmean = pl.pallas_call(
        functools.partial(_moba_seq_kernel, nblk=nblk, blk=MOBA_BLOCK,
                          n_sel=min(MOBA_TOP_K, nblk - 1), scale=hd ** -0.5, per=per, npp=npp),
        grid_spec=grid_spec,
        out_shape=[jax.ShapeDtypeStruct((b, t, d), BF16), km_shape],
        compiler_params=_params(("parallel", "parallel"), block_bytes),
        name="moba_seq",
    )(page_table.reshape(-1), q, k, v, *([cache_k] * (per * npp)))
    return o, kmean[:cache_blocks]


def _block_select_kernel(q_ref, km_ref, sel_ref, *, n_sel):
    q = q_ref[0]
    km = km_ref[0]
    nb = km.shape[0]
    g = jnp.sum(km * q[None], axis=-1)
    blk_i = lax.broadcasted_iota(jnp.int32, g.shape, 0)
    rows = []
    for _ in range(n_sel):
        m = jnp.max(g, axis=0, keepdims=True)
        idx = jnp.min(jnp.where(g == m, blk_i, nb), axis=0, keepdims=True)
        rows.append(idx)
        g = jnp.where(blk_i == idx, -jnp.inf, g)
    sel_ref[0] = jnp.concatenate(rows, axis=0)


def _block_select(q, kmean, n_sel):
    s, nb, n_heads, hd = kmean.shape
    return pl.pallas_call(
        functools.partial(_block_select_kernel, n_sel=n_sel),
        grid=(s,),
        in_specs=[pl.BlockSpec((1, n_heads, hd), lambda si: (si, 0, 0)),
                  pl.BlockSpec((1, nb, n_heads, hd), lambda si: (si, 0, 0, 0))],
        out_specs=pl.BlockSpec((1, n_sel, n_heads), lambda si: (si, 0, 0)),
        out_shape=jax.ShapeDtypeStruct((s, n_sel, n_heads), jnp.int32),
        name="block_select",
    )(q, kmean)


def _moba_step_kernel(pt_ref, sel_ref, q_ref, kn_ref, vn_ref, ck_hbm, cv_hbm, o_ref, kbuf, vbuf,
                      sems, *, n_sel, npp, npages, scale):
    si, ns = pl.program_id(0), pl.num_programs(0)
    n_heads = q_ref.shape[1]
    page = ck_hbm.shape[1]
    group = n_heads // 2

    def gather(seq, grp):
        copies = []
        for hl in range(group):
            h = grp * group + hl
            for r in range(n_sel):
                blk = sel_ref[(seq * n_sel + r) * n_heads + h]
                for p in range(npp):
                    pg = pt_ref[seq * npages + blk * npp + p]
                    rows = pl.ds((r * npp + p) * page, page)
                    copies.append(pltpu.make_async_copy(
                        ck_hbm.at[pg, :, h, :], kbuf.at[grp, hl, rows, :], sems.at[0, grp]))
                    copies.append(pltpu.make_async_copy(
                        cv_hbm.at[pg, :, h, :], vbuf.at[grp, hl, rows, :], sems.at[1, grp]))
        return copies

    def attend(grp):
        for hl in range(group):
            h = grp * group + hl
            q = q_ref[0, h:h + 1, :]
            s = jnp.sum(kbuf[grp, hl] * q, axis=-1, keepdims=True) * scale
            s_own = jnp.sum(kn_ref[0, h:h + 1, :] * q, axis=-1, keepdims=True) * scale
            m = jnp.maximum(jnp.max(s, axis=0, keepdims=True), s_own)
            p = jnp.exp(s - m)
            p_own = jnp.exp(s_own - m)
            l = jnp.sum(p, axis=0, keepdims=True) + p_own
            o = jnp.sum(p * vbuf[grp, hl], axis=0, keepdims=True) + p_own * vn_ref[0, h:h + 1, :]
            o_ref[0, h:h + 1, :] = (o / l).astype(o_ref.dtype)

    @pl.when(si == 0)
    def _():
        for grp in range(2):
            for cp in gather(si, grp):
                cp.start()

    for grp in range(2):
        for cp in gather(si, grp):
            cp.wait()
        attend(grp)

        @pl.when(si + 1 < ns)
        def _():
            for cp in gather(si + 1, grp):
                cp.start()


def _moba_step(q, k_new, v_new, cache_k, cache_v, page_table, sel):
    s, n_heads, hd = q.shape
    _, page, _, _ = cache_k.shape
    npages = page_table.shape[1]
    n_sel = sel.shape[1]
    npp = MOBA_BLOCK // page
    assert n_heads % 2 == 0
    keys = n_sel * MOBA_BLOCK
    row = pl.BlockSpec((1, n_heads, hd), lambda si, pt, sl: (si, 0, 0))
    grid_spec = pltpu.PrefetchScalarGridSpec(
        num_scalar_prefetch=2,
        grid=(s,),
        in_specs=[row, row, row, pl.BlockSpec(memory_space=pl.ANY),
                  pl.BlockSpec(memory_space=pl.ANY)],
        out_specs=row,
        scratch_shapes=[pltpu.VMEM((2, n_heads // 2, keys, hd), cache_k.dtype),
                        pltpu.VMEM((2, n_heads // 2, keys, hd), cache_v.dtype),
                        pltpu.SemaphoreType.DMA((2, 2))],
    )
    scratch_bytes = 2 * _nbytes((n_heads, keys, hd), cache_k.dtype)
    out = pl.pallas_call(
        functools.partial(_moba_step_kernel, n_sel=n_sel, npp=npp, npages=npages, scale=hd ** -0.5),
        grid_spec=grid_spec,
        out_shape=jax.ShapeDtypeStruct((s, n_heads, hd), F32),
        compiler_params=_params(("arbitrary",), 4 * _nbytes((n_heads, hd), F32), scratch_bytes),
        name="moba_step",
    )(page_table.reshape(-1), sel.reshape(-1), q, k_new, v_new, cache_k, cache_v)
    return out.reshape(s, n_heads * hd)


def _swiglu_block(hp, hs, norm_gain, w_gate, w_up, w_down, layer, tag):
    (xp,) = _rmsnorm(hp, [norm_gain], [BF16], f"ffn_norm_p{tag}")
    (xs,) = _rmsnorm(hs, [norm_gain], [BF16], f"ffn_norm_s{tag}")
    (ap,), (as_,) = _matmul(xp, xs, [(w_gate, layer), (w_up, layer)], _ep_swiglu, [BF16],
                            tm=1024, tn=512, name=f"ffn_gate_up{tag}")
    wd = [(w_down, layer)]
    (pp,), (ps,) = _matmul(ap, as_, wd, _ep_identity, [F32], tm=1024, tn=512, k_part=(0, 2),
                           name=f"ffn_down_a{tag}")
    (hp,), (hs,) = _matmul(ap, as_, wd, _ep_residual_partial, [F32], [pp, hp], [ps, hs],
                           tm=1024, tn=512, k_part=(1, 2), name=f"ffn_down_b{tag}")
    return hp, hs


def kernel(x_prompt, x_sample, cache_k, cache_v, page_table, state_conv, state_h, mixer_norm, lru_w_x, lru_w_y, lru_conv_w, lru_conv_b, lru_w_a, lru_b_a, lru_w_i, lru_b_i, lru_lambda, lru_w_out, kv_norm, w_k, w_v, w_q, w_o, ffn_norm, w_gate, w_up, w_down, final_norm):
    bp, tp, d = x_prompt.shape
    bs, ts, _ = x_sample.shape
    n_heads, hd = cache_k.shape[2], cache_k.shape[3]
    page = cache_k.shape[1]
    past_len = page_table.shape[1] * page
    depth = mixer_norm.shape[0]
    n_a = lru_w_x.shape[0]
    assert ts == 1, "the decode group advances one position per step"
    assert MOBA_BLOCK % page == 0 and past_len % MOBA_BLOCK == 0
    n_past = past_len // MOBA_BLOCK
    n_sel_s = min(MOBA_TOP_K, n_past)
    assert n_past >= n_sel_s and 0 < n_a < depth
    n_blocks = bs * n_past
    lru_blocks = n_blocks // 4

    hp = x_prompt.reshape(bp * tp, d)
    hs = x_sample.reshape(bs * ts, d)
    conv_p, conv_s, hl_p, hl_s = [], [], [], []
    k_p = v_p = k_s = v_s = kmean_lru = kmean_s = None
    (xp,) = _rmsnorm(hp, [mixer_norm[0]], [BF16], "mixer_norm_p0")
    (xs,) = _rmsnorm(hs, [mixer_norm[0]], [BF16], "mixer_norm_s0")

    for layer in range(depth):
        if layer < n_a:
            a = layer
            lru = (lru_conv_w[a], lru_conv_b[a], lru_w_a[a], lru_b_a[a], lru_w_i[a], lru_b_i[a],
                   lru_lambda[a])
            cwid = lru_conv_w.shape[1]
            (xr_p, gate_p), (xr_s, gate_s) = _matmul(
                xp, xs, [(lru_w_x, a), (lru_w_y, a)], _ep_x_gelu, [F32, F32], tm=1024, tn=512,
                name=f"lru_in{layer}")
            w = xr_p.shape[1]
            xr3 = xr_p.reshape(bp, tp, w)
            hg_p, hlast_p, km = _rglru_seq(
                xr3, gate_p.reshape(bp, tp, w), jnp.zeros((bp, cwid - 1, w), F32),
                jnp.zeros((bp, w), F32), *lru, cache_k, page_table,
                lru_blocks if layer == 0 else 1, first_pos_zero=True)
            if layer == 0:
                kmean_lru = km
            hg_s, hlast_s, nbuf = _rglru_step(xr_s, gate_s, state_conv[a], state_h[a], *lru,
                                              first_pos_zero=(past_len == 0))
            (hp,), (hs,) = _matmul(hg_p.reshape(bp * tp, w), hg_s, [(lru_w_out, a)], _ep_residual,
                                   [F32], [hp], [hs], tm=1024, tn=512, name=f"lru_out{layer}")
            conv_p.append(xr3[:, tp - (cwid - 1):, :])
            conv_s.append(nbuf)
            hl_p.append(hlast_p)
            hl_s.append(hlast_s)
        else:
            b = layer - n_a
            (q_p,), (q_s,) = _matmul(xp, xs, [(w_q, b)], _ep_identity, [F32], tm=1024, tn=512,
                                     name=f"attn_q{layer}")
            first = lru_blocks if layer == n_a else 0
            o_p, km = _moba_seq(q_p.reshape(bp, tp, d), k_p.reshape(bp, tp, d),
                                v_p.reshape(bp, tp, d), n_heads, cache_k, page_table, first,
                                n_blocks - first if layer == n_a else 1)
            if layer == n_a:
                kmean_s = jnp.concatenate([kmean_lru, km], axis=0).reshape(bs, n_past, n_heads, hd)
            q3 = q_s.reshape(bs, n_heads, hd)
            sel = _block_select(q3, kmean_s, n_sel_s)
            o_s = _moba_step(q3, k_s.reshape(bs, n_heads, hd), v_s.reshape(bs, n_heads, hd),
                             cache_k, cache_v, page_table, sel)
            (hp,), (hs,) = _matmul(o_p.reshape(bp * tp, d), o_s.astype(BF16), [(w_o, b)],
                                   _ep_residual, [F32], [hp], [hs], tm=1024, tn=512,
                                   name=f"attn_o{layer}")

        hp, hs = _swiglu_block(hp, hs, ffn_norm[layer], w_gate, w_up, w_down, layer, str(layer))

        gains = []
        if layer == n_a - 1:
            gains.append(kv_norm)
        gains.append(mixer_norm[layer + 1] if layer + 1 < depth else final_norm)
        outs_p = _rmsnorm(hp, gains, [BF16] * (len(gains) - 1) + [BF16 if layer + 1 < depth else F32],
                          f"stream_norm_p{layer}")
        outs_s = _rmsnorm(hs, gains, [BF16] * (len(gains) - 1) + [BF16 if layer + 1 < depth else F32],
                          f"stream_norm_s{layer}")
        if layer == n_a - 1:
            (k_p, v_p), (k_s, v_s) = _matmul(
                outs_p[0], outs_s[0],
                [(w_k.reshape(1, *w_k.shape), 0), (w_v.reshape(1, *w_v.shape), 0)],
                _ep_identity, [F32, F32], tm=1024, tn=512, name="kv")
        xp, xs = outs_p[-1], outs_s[-1]

    y_p, y_s = xp, xs
    return (y_p.reshape(bp, tp, d), y_s.reshape(bs, ts, d),
            k_p.reshape(bp, tp, n_heads, hd), v_p.reshape(bp, tp, n_heads, hd),
            k_s.reshape(bs, ts, n_heads, hd), v_s.reshape(bs, ts, n_heads, hd),
            jnp.stack(conv_p), jnp.stack(conv_s), jnp.stack(hl_p), jnp.stack(hl_s))
```

```python
import functools

import jax
import jax.numpy as jnp
from jax import lax
from jax.experimental import pallas as pl
from jax.experimental.pallas import tpu as pltpu

F32 = jnp.float32
BF16 = jnp.bfloat16

RMS_EPS = 1e-6
LRU_C = 8.0
MOBA_BLOCK = 256
MOBA_TOP_K = 3
LOG2_E = 1.4426950408889634

V7X_VMEM_BYTES = 64 * 1024 * 1024
V7X_SUBLANES = 8
V7X_LANES = 128
_VMEM_INTERNAL_BYTES = 16 * 1024 * 1024


def _pick_tile(dim, target, align):
    best = None
    d = align
    while d <= min(dim, target):
        if dim % d == 0:
            best = d
        d += align
    return best if best is not None else dim


def _nbytes(shape, dtype):
    n = 1
    for s in shape:
        n *= s
    return n * jnp.dtype(dtype).itemsize


def _params(semantics, block_bytes, scratch_bytes=0):
    limit = 2 * block_bytes + scratch_bytes + _VMEM_INTERNAL_BYTES
    limit = min(max(limit, 16 * 1024 * 1024), V7X_VMEM_BYTES - 4 * 1024 * 1024)
    return pltpu.CompilerParams(dimension_semantics=semantics, vmem_limit_bytes=int(limit))


def _rmsnorm_kernel(x_ref, *refs, n_out):
    g_refs, o_refs = refs[:n_out], refs[n_out:]
    x = x_ref[...].astype(F32)
    y = x * lax.rsqrt(jnp.mean(x * x, axis=-1, keepdims=True) + RMS_EPS)
    for g_ref, o_ref in zip(g_refs, o_refs):
        o_ref[...] = (y * g_ref[...].astype(F32)).astype(o_ref.dtype)


def _rmsnorm(x, gains, out_dtypes, name):
    m, d = x.shape
    tm = _pick_tile(m, 256, V7X_SUBLANES)
    n_out = len(gains)
    block_bytes = _nbytes((tm, d), x.dtype) + sum(_nbytes((tm, d), dt) for dt in out_dtypes)
    outs = pl.pallas_call(
        functools.partial(_rmsnorm_kernel, n_out=n_out),
        grid=(m // tm,),
        in_specs=[pl.BlockSpec((tm, d), lambda i: (i, 0))]
        + [pl.BlockSpec((1, d), lambda i: (0, 0))] * n_out,
        out_specs=[pl.BlockSpec((tm, d), lambda i: (i, 0))] * n_out,
        out_shape=[jax.ShapeDtypeStruct((m, d), dt) for dt in out_dtypes],
        compiler_params=_params(("parallel",), block_bytes),
        name=name,
    )(x, *[g.reshape(1, d) for g in gains])
    return outs


def _matmul_kernel(*refs, n_w, n_extra, n_out, epilogue, nj, ck, tm, n_cols, has_gain, has_scale):
    ap_ref, as_ref = refs[0], refs[1]
    pos = 2
    w_refs = refs[pos:pos + n_w]
    pos += n_w
    ep_refs = refs[pos:pos + n_extra]
    pos += n_extra
    es_refs = refs[pos:pos + n_extra]
    pos += n_extra
    g_ref = rp_ref = rs_ref = None
    if has_gain:
        g_ref = refs[pos]
        pos += 1
    if has_scale:
        rp_ref, rs_ref = refs[pos], refs[pos + 1]
        pos += 2
    op_refs = refs[pos:pos + n_out]
    pos += n_out
    os_refs = refs[pos:pos + n_out]
    pos += n_out
    hbp_ref = hbs_ref = rop_ref = ros_ref = ssqp_ref = ssqs_ref = None
    if has_gain:
        hbp_ref, hbs_ref, rop_ref, ros_ref = refs[pos:pos + 4]
        pos += 4
    wb_refs = refs[pos:pos + n_w]
    pos += n_w
    if has_gain:
        ssqp_ref, ssqs_ref = refs[pos], refs[pos + 1]
    s, i = pl.program_id(0), pl.program_id(1)
    staged = s % 2
    ready = (s + 1) % 2
    rows_p = pl.ds(pl.multiple_of(i * tm, tm), tm)
    tn = wb_refs[0].shape[2]

    def stage():
        row = pl.multiple_of(i * ck, ck)
        for w_ref, wb_ref in zip(w_refs, wb_refs):
            wb_ref[staged, pl.ds(row, ck), :] = w_ref[...].astype(BF16)

    def apply(a_ref, e_refs, o_refs, r_ref, hb_ref, ro_ref, ssq_ref, ssq_rows, last):
        a = a_ref[...]
        accs = [jnp.dot(a, wb_ref[ready], preferred_element_type=F32) for wb_ref in wb_refs]
        if has_scale:
            scale = jnp.tile(r_ref[...], (1, tn // V7X_LANES))
            accs = [acc * scale for acc in accs]
        outs = epilogue(*accs, *[e[...] for e in e_refs])
        for o_ref, val in zip(o_refs, outs):
            o_ref[...] = val.astype(o_ref.dtype)
        if has_gain:
            h = outs[0]
            hb_ref[...] = (h * g_ref[...]).astype(hb_ref.dtype)
            sq = h * h
            part = sq[:, 0:V7X_LANES]
            for c in range(1, tn // V7X_LANES):
                part = part + sq[:, c * V7X_LANES:(c + 1) * V7X_LANES]
            total = ssq_ref[ssq_rows, :] + part
            ssq_ref[ssq_rows, :] = total
            if last:
                mean = jnp.sum(total, axis=-1, keepdims=True) / n_cols
                ro_ref[...] = jnp.broadcast_to(lax.rsqrt(mean + RMS_EPS), total.shape)

    def multiply(last):
        apply(ap_ref, ep_refs, op_refs, rp_ref, hbp_ref, rop_ref, ssqp_ref, rows_p, last)

        @pl.when(i == 0)
        def _():
            apply(as_ref, es_refs, os_refs, rs_ref, hbs_ref, ros_ref, ssqs_ref, slice(None), last)

    @pl.when(s == 0)
    def _():
        stage()
        if has_gain:
            ssqp_ref[rows_p, :] = jnp.zeros((tm, V7X_LANES), F32)
            ssqs_ref[...] = jnp.zeros_like(ssqs_ref)

    @pl.when((s > 0) & (s < nj))
    def _():
        stage()
        multiply(False)

    @pl.when(s == nj)
    def _():
        multiply(True)


def _matmul(a_p, a_s, ws, epilogue, out_dtypes, extras_p=(), extras_s=(), *, tm, tn,
            k_part=(0, 1), gain=None, row_scale=None, name):
    m, kdim = a_p.shape
    s_rows = a_s.shape[0]
    n = ws[0][0].shape[2]
    kp, nkp = k_part
    assert kdim % nkp == 0
    tk = kdim // nkp
    tm = _pick_tile(m, tm, V7X_SUBLANES)
    ni = m // tm
    n_w, n_extra, n_out = len(ws), len(extras_p), len(out_dtypes)
    assert len(extras_s) == n_extra
    tn = min(tn, n)
    assert tn % V7X_LANES == 0
    nj = pl.cdiv(n, tn)
    assert tk % ni == 0 and (tk // ni) % (2 * V7X_SUBLANES) == 0
    ck = tk // ni
    has_gain, has_scale = gain is not None, row_scale is not None
    assert not has_gain or n % tn == 0

    def col(s):
        return jnp.maximum(s - 1, 0)

    def row(s, i):
        return jnp.where(s == 0, 0, i)

    def w_spec(layer):
        def index(s, i):
            return (layer, kp * ni + jnp.where(s == nj, ni - 1, i), jnp.minimum(s, nj - 1))
        return pl.BlockSpec((None, ck, tn), index)

    p_tile = pl.BlockSpec((tm, tn), lambda s, i: (row(s, i), col(s)))
    s_tile = pl.BlockSpec((s_rows, tn), lambda s, i: (0, col(s)))
    lanes_s = pl.BlockSpec((s_rows, V7X_LANES), lambda s, i: (0, 0))
    in_specs = ([pl.BlockSpec((tm, tk), lambda s, i: (row(s, i), kp)),
                 pl.BlockSpec((s_rows, tk), lambda s, i: (0, kp))]
                + [w_spec(layer) for _, layer in ws]
                + [p_tile] * n_extra + [s_tile] * n_extra)
    operands = [a_p, a_s, *[w for w, _ in ws], *extras_p, *extras_s]
    out_specs = [p_tile] * n_out + [s_tile] * n_out
    out_shape = ([jax.ShapeDtypeStruct((m, n), dt) for dt in out_dtypes]
                 + [jax.ShapeDtypeStruct((s_rows, n), dt) for dt in out_dtypes])
    scratch = [pltpu.VMEM((2, tk, tn), BF16) for _ in ws]
    scratch_bytes = 2 * n_w * _nbytes((tk, tn), BF16)
    block_bytes = (_nbytes((tm + s_rows, tk), BF16) + n_w * _nbytes((ck, tn), F32)
                   + sum(_nbytes((tm + s_rows, tn), e.dtype) for e in extras_p)
                   + sum(_nbytes((tm + s_rows, tn), dt) for dt in out_dtypes))
    if has_gain:
        in_specs.append(pl.BlockSpec((1, tn), lambda s, i: (0, col(s))))
        operands.append(gain.reshape(1, n).astype(F32))
        out_specs += [p_tile, s_tile,
                      pl.BlockSpec((tm, V7X_LANES), lambda s, i: (jnp.where(s == nj, i, 0), 0)),
                      lanes_s]
        out_shape += [jax.ShapeDtypeStruct((m, n), BF16), jax.ShapeDtypeStruct((s_rows, n), BF16),
                      jax.ShapeDtypeStruct((m, V7X_LANES), F32),
                      jax.ShapeDtypeStruct((s_rows, V7X_LANES), F32)]
        scratch += [pltpu.VMEM((m, V7X_LANES), F32), pltpu.VMEM((s_rows, V7X_LANES), F32)]
        scratch_bytes += _nbytes((m + s_rows, V7X_LANES), F32)
        block_bytes += _nbytes((tm + s_rows, tn), BF16) + _nbytes((tm + s_rows, V7X_LANES), F32)
    if has_scale:
        in_specs += [pl.BlockSpec((tm, V7X_LANES), lambda s, i: (row(s, i), 0)), lanes_s]
        operands += list(row_scale)
        block_bytes += _nbytes((tm + s_rows, V7X_LANES), F32)
    outs = pl.pallas_call(
        functools.partial(_matmul_kernel, n_w=n_w, n_extra=n_extra, n_out=n_out, epilogue=epilogue,
                          nj=nj, ck=ck, tm=tm, n_cols=n, has_gain=has_gain, has_scale=has_scale),
        grid=(nj + 1, ni),
        in_specs=in_specs,
        out_specs=out_specs,
        out_shape=out_shape,
        scratch_shapes=scratch,
        compiler_params=_params(("arbitrary", "arbitrary"), block_bytes, scratch_bytes),
        name=name,
    )(*operands)
    if has_gain:
        hb_p, hb_s, r_p, r_s = outs[2 * n_out:]
        return outs[:n_out], outs[n_out:2 * n_out], ((hb_p, r_p), (hb_s, r_s))
    return outs[:n_out], outs[n_out:]


def _ep_identity(*accs):
    return accs


def _ep_x_gelu(acc_x, acc_y):
    return acc_x, jax.nn.gelu(acc_y)


def _ep_residual(acc, res):
    return (res + acc,)


def _ep_residual_partial(acc, partial, res):
    return (res + (partial + acc),)


def _ep_swiglu(acc_gate, acc_up):
    return (jax.nn.silu(acc_gate) * acc_up,)


def _cache_mean_specs(cache_k, page_table, first, count, steps, step_of):
    _, page, c_heads, c_hd = cache_k.shape
    npp = MOBA_BLOCK // page
    per = -(-count // steps)

    def page_spec(it, p):
        def index(*args):
            *grid_idx, pt = args
            item = first + jnp.minimum(step_of(*grid_idx) * per + it, count - 1)
            return (pt[item * npp + p], 0, 0, 0)
        return pl.BlockSpec((1, page, c_heads, c_hd), index)

    specs = [page_spec(it, p) for it in range(per) for p in range(npp)]
    out_spec = pl.BlockSpec((per, c_heads, c_hd), lambda *args: (step_of(*args[:-1]), 0, 0))
    out_shape = jax.ShapeDtypeStruct((steps * per, c_heads, c_hd), F32)
    return per, specs, out_spec, out_shape


def _reduce_cache_blocks(page_refs, km_ref, per, npp):
    for it in range(per):
        acc = jnp.sum(page_refs[it * npp][0].astype(F32), axis=0)
        for p in range(1, npp):
            acc = acc + jnp.sum(page_refs[it * npp + p][0].astype(F32), axis=0)
        km_ref[it] = acc / MOBA_BLOCK


def _lru_gates(xc, head, wa_ref, ba_ref, wi_ref, bi_ref, lam):
    xb = xc.astype(BF16)
    r = jax.nn.sigmoid(jnp.dot(xb, wa_ref[head], preferred_element_type=F32) + ba_ref[head])
    i = jax.nn.sigmoid(jnp.dot(xb, wi_ref[head], preferred_element_type=F32) + bi_ref[head])
    log_a = LRU_C * r * jax.nn.log_sigmoid(lam)
    a = jnp.exp(log_a)
    mult = jnp.sqrt(-jnp.tanh(log_a) * (a * a + 1.0))
    return a, mult, i


def _rglru_seq_kernel(pt_ref, xr_ref, gate_ref, cbuf_ref, h0_ref, cw_ref, cb_ref, wa_ref, ba_ref,
                      wi_ref, bi_ref, lam_ref, *refs, tt, nh, dh, cwid, first_pos_zero, per, npp):
    n_pg = per * npp
    page_refs = refs[:n_pg]
    hg_ref, hlast_ref, km_ref, xp_scr, a_scr, u_scr, hs_scr, h_scr = refs[n_pg:]
    _reduce_cache_blocks(page_refs, km_ref, per, npp)
    t = pl.program_id(2)
    halo = cwid - 1
    base = V7X_SUBLANES - halo

    @pl.when(t == 0)
    def _():
        xp_scr[base:V7X_SUBLANES, :] = cbuf_ref[0].astype(F32)
        h_scr[...] = h0_ref[0].astype(F32)

    xp_scr[V7X_SUBLANES:V7X_SUBLANES + tt, :] = xr_ref[0]
    xc = cb_ref[...] + cw_ref[0:1, :] * xp_scr[base:base + tt, :]
    for j in range(1, cwid):
        xc = xc + cw_ref[j:j + 1, :] * xp_scr[base + j:base + j + tt, :]
    xp_scr[base:V7X_SUBLANES, :] = xp_scr[base + tt:V7X_SUBLANES + tt, :]

    for hd in range(nh):
        sl = slice(hd * dh, (hd + 1) * dh)
        xh = xc[:, sl]
        a, mult, i = _lru_gates(xh, hd, wa_ref, ba_ref, wi_ref, bi_ref, lam_ref[:, sl])
        if first_pos_zero:
            row = lax.broadcasted_iota(jnp.int32, mult.shape, 0)
            mult = jnp.where((row == 0) & (t == 0), 1.0, mult)
        a_scr[:, sl] = a
        u_scr[:, sl] = mult * i * xh

    def step(s, h):
        h = a_scr[pl.ds(s, 1), :] * h + u_scr[pl.ds(s, 1), :]
        hs_scr[pl.ds(s, 1), :] = h
        return h

    h = lax.fori_loop(0, tt, step, h_scr[...], unroll=8)
    h_scr[...] = h
    hg_ref[0] = (hs_scr[...] * gate_ref[0]).astype(hg_ref.dtype)
    hlast_ref[0] = h


def _rglru_seq(xr, gate, conv_buf, h0, conv_w, conv_b, w_a, b_a, w_i, b_i, lam, cache_k, page_table,
               cache_blocks, *, first_pos_zero):
    b, t, w = xr.shape
    nheads, dh = w_a.shape[0], w_a.shape[1]
    cwid = conv_w.shape[0]
    wc = _pick_tile(w, 1024, dh)
    nh = wc // dh
    tt = _pick_tile(t, 256, V7X_SUBLANES)
    n_c, n_t = w // wc, t // tt
    blk = lambda bi_, ci, ti, *_: (bi_, ti, ci)
    per_bc = lambda bi_, ci, ti, *_: (bi_, 0, ci)
    chan = lambda bi_, ci, ti, *_: (0, ci)
    head = lambda bi_, ci, ti, *_: (ci, 0, 0)
    npp = MOBA_BLOCK // cache_k.shape[1]
    per, page_specs, km_spec, km_shape = _cache_mean_specs(
        cache_k, page_table, 0, cache_blocks, b * n_c * n_t,
        lambda bi_, ci, ti: (bi_ * n_c + ci) * n_t + ti)
    block_bytes = (2 * _nbytes((tt, wc), F32) + _nbytes((tt, wc), BF16)
                   + 2 * _nbytes((nh, dh, dh), BF16) + 16 * _nbytes((1, wc), F32)
                   + per * npp * _nbytes(cache_k.shape[1:], cache_k.dtype))
    scratch_bytes = 4 * _nbytes((tt + V7X_SUBLANES, wc), F32)
    hg, hlast, kmean = pl.pallas_call(
        functools.partial(_rglru_seq_kernel, tt=tt, nh=nh, dh=dh, cwid=cwid,
                          first_pos_zero=first_pos_zero, per=per, npp=npp),
        grid_spec=pltpu.PrefetchScalarGridSpec(
            num_scalar_prefetch=1,
            grid=(b, n_c, n_t),
            in_specs=[
                pl.BlockSpec((1, tt, wc), blk),
                pl.BlockSpec((1, tt, wc), blk),
                pl.BlockSpec((1, cwid - 1, wc), per_bc),
                pl.BlockSpec((1, 1, wc), per_bc),
                pl.BlockSpec((cwid, wc), chan),
                pl.BlockSpec((1, wc), chan),
                pl.BlockSpec((nh, dh, dh), head),
                pl.BlockSpec((nh, 1, dh), head),
                pl.BlockSpec((nh, dh, dh), head),
                pl.BlockSpec((nh, 1, dh), head),
                pl.BlockSpec((1, wc), chan),
            ] + page_specs,
            out_specs=[pl.BlockSpec((1, tt, wc), blk), pl.BlockSpec((1, 1, wc), per_bc), km_spec],
            scratch_shapes=[
                pltpu.VMEM((tt + V7X_SUBLANES, wc), F32),
                pltpu.VMEM((tt, wc), F32),
                pltpu.VMEM((tt, wc), F32),
                pltpu.VMEM((tt, wc), F32),
                pltpu.VMEM((1, wc), F32),
            ]),
        out_shape=[jax.ShapeDtypeStruct((b, t, w), BF16), jax.ShapeDtypeStruct((b, 1, w), F32),
                   km_shape],
        compiler_params=_params(("parallel", "parallel", "arbitrary"), block_bytes, scratch_bytes),
        name="rglru_seq",
    )(page_table.reshape(-1), xr, gate, conv_buf, h0.reshape(b, 1, w), conv_w, conv_b.reshape(1, w),
      w_a.astype(BF16), b_a.reshape(nheads, 1, dh), w_i.astype(BF16), b_i.reshape(nheads, 1, dh),
      lam.reshape(1, w), *([cache_k] * (per * npp)))
    return hg, hlast.reshape(b, w), kmean[:cache_blocks]


def _rglru_step_kernel(xr_ref, gate_ref, cbuf_ref, h0_ref, cw_ref, cb_ref, wa_ref, ba_ref, wi_ref,
                       bi_ref, lam_ref, hg_ref, h_ref, nbuf_ref, *, nheads, dh, cwid,
                       first_pos_zero):
    xr = xr_ref[...]
    xc = cb_ref[...] + cw_ref[cwid - 1:cwid, :] * xr
    for j in range(cwid - 1):
        xc = xc + cw_ref[j:j + 1, :] * cbuf_ref[j]
    for j in range(cwid - 2):
        nbuf_ref[j] = cbuf_ref[j + 1]
    nbuf_ref[cwid - 2] = xr
    for hd in range(nheads):
        sl = slice(hd * dh, (hd + 1) * dh)
        xh = xc[:, sl]
        a, mult, i = _lru_gates(xh, hd, wa_ref, ba_ref, wi_ref, bi_ref, lam_ref[:, sl])
        if first_pos_zero:
            mult = jnp.ones_like(mult)
        h = a * h0_ref[:, sl] + mult * i * xh
        h_ref[:, sl] = h
        hg_ref[:, sl] = (h * gate_ref[:, sl]).astype(hg_ref.dtype)


def _rglru_step(xr, gate, conv_buf, h0, conv_w, conv_b, w_a, b_a, w_i, b_i, lam, *, first_pos_zero):
    s, w = xr.shape
    nheads, dh = w_a.shape[0], w_a.shape[1]
    cwid = conv_w.shape[0]
    hg, h, nbuf = pl.pallas_call(
        functools.partial(_rglru_step_kernel, nheads=nheads, dh=dh, cwid=cwid,
                          first_pos_zero=first_pos_zero),
        out_shape=[jax.ShapeDtypeStruct((s, w), BF16), jax.ShapeDtypeStruct((s, w), F32),
                   jax.ShapeDtypeStruct((cwid - 1, s, w), F32)],
        name="rglru_step",
    )(xr, gate, jnp.swapaxes(conv_buf, 0, 1), h0, conv_w, conv_b.reshape(1, w),
      w_a.astype(BF16), b_a.reshape(nheads, 1, dh), w_i.astype(BF16), b_i.reshape(nheads, 1, dh),
      lam.reshape(1, w))
    return hg, h, jnp.swapaxes(nbuf, 0, 1)


def _moba_seq_kernel(pt_ref, q_ref, k_ref, v_ref, *refs, nblk, blk, n_sel, scale, per, npp):
    page_refs, o_ref, km_ref = refs[:per * npp], refs[per * npp], refs[per * npp + 1]
    _reduce_cache_blocks(page_refs, km_ref, per, npp)

    k32 = k_ref[0]
    kb = k32.astype(BF16)
    v_t = v_ref[0].T.astype(BF16)
    nblk_pad = -(-nblk // V7X_SUBLANES) * V7X_SUBLANES
    means = [jnp.mean(k32[b * blk:(b + 1) * blk], axis=0, keepdims=True) for b in range(nblk)]
    means += [jnp.zeros_like(means[0])] * (nblk_pad - nblk)
    kmean = jnp.concatenate(means, axis=0)
    key_i = lax.broadcasted_iota(jnp.int32, (blk, blk), 0)
    qry_i = lax.broadcasted_iota(jnp.int32, (blk, blk), 1)
    causal = key_i <= qry_i
    neg_inf = jnp.float32(-jnp.inf)
    contract_last = (((1,), (1,)), ((), ()))
    exp2_scale = scale * LOG2_E

    for c in range(nblk):
        q32 = q_ref[0, c * blk:(c + 1) * blk, :]
        qb = q32.astype(BF16)
        sels = [None] * c
        if c > n_sel:
            gate = lax.dot_general(kmean, q32, contract_last, precision=lax.Precision.HIGHEST,
                                   preferred_element_type=F32)
            g = [gate[b:b + 1, :] for b in range(c)]
            for b in range(c):
                cnt = jnp.zeros((1, blk), jnp.int32)
                for b2 in range(c):
                    if b2 == b:
                        continue
                    beats = (g[b2] >= g[b]) if b2 < b else (g[b2] > g[b])
                    cnt = cnt + beats.astype(jnp.int32)
                sels[b] = cnt < n_sel
        s_blocks = []
        for b in range(c + 1):
            s = lax.dot_general(kb[b * blk:(b + 1) * blk], qb, contract_last,
                                preferred_element_type=F32)
            if b == c:
                s = jnp.where(causal, s, neg_inf)
            elif sels[b] is not None:
                s = jnp.where(sels[b], s, neg_inf)
            s_blocks.append(s)
        m = jnp.max(s_blocks[c], axis=0, keepdims=True)
        for b in range(c):
            m = jnp.maximum(m, jnp.max(s_blocks[b], axis=0, keepdims=True))
        l = jnp.zeros((1, blk), F32)
        o_t = jnp.zeros((v_t.shape[0], blk), F32)
        for b in range(c + 1):
            p = jnp.exp2((s_blocks[b] - m) * exp2_scale)
            l = l + jnp.sum(p, axis=0, keepdims=True)
            o_t = o_t + jnp.dot(v_t[:, b * blk:(b + 1) * blk], p.astype(BF16),
                                preferred_element_type=F32)
        o_ref[0, c * blk:(c + 1) * blk, :] = (o_t / l).T.astype(o_ref.dtype)


def _moba_seq(q, k, v, n_heads, cache_k, page_table, first_block, cache_blocks):
    b, t, d = q.shape
    hd = d // n_heads
    assert t % MOBA_BLOCK == 0 and hd % V7X_LANES == 0
    nblk = t // MOBA_BLOCK
    npp = MOBA_BLOCK // cache_k.shape[1]
    per, page_specs, km_spec, km_shape = _cache_mean_specs(
        cache_k, page_table, first_block, cache_blocks, b * n_heads,
        lambda bi, hi: bi * n_heads + hi)
    spec = pl.BlockSpec((1, t, hd), lambda bi, hi, pt: (bi, 0, hi))
    grid_spec = pltpu.PrefetchScalarGridSpec(
        num_scalar_prefetch=1,
        grid=(b, n_heads),
        in_specs=[spec, spec, spec] + page_specs,
        out_specs=[spec, km_spec],
    )
    block_bytes = (3 * _nbytes((t, hd), F32) + _nbytes((t, hd), BF16)
                   + per * npp * _nbytes(cache_k.shape[1:], cache_k.dtype)
                   + per * _nbytes(cache_k.shape[2:], F32))
    o, kmean = pl.pallas_call(
        functools.partial(_moba_seq_kernel, nblk=nblk, blk=MOBA_BLOCK,
                          n_sel=min(MOBA_TOP_K, nblk - 1), scale=hd ** -0.5, per=per, npp=npp),
        grid_spec=grid_spec,
        out_shape=[jax.ShapeDtypeStruct((b, t, d), BF16), km_shape],
        compiler_params=_params(("parallel", "parallel"), block_bytes),
        name="moba_seq",
    )(page_table.reshape(-1), q, k, v, *([cache_k] * (per * npp)))
    return o, kmean[:cache_blocks]


def _block_select_kernel(q_ref, km_ref, sel_ref, *, n_sel):
    q = q_ref[0]
    km = km_ref[0]
    nb = km.shape[0]
    g = jnp.sum(km * q[None], axis=-1)
    blk_i = lax.broadcasted_iota(jnp.int32, g.shape, 0)
    rows = []
    for _ in range(n_sel):
        m = jnp.max(g, axis=0, keepdims=True)
        idx = jnp.min(jnp.where(g == m, blk_i, nb), axis=0, keepdims=True)
        rows.append(idx)
        g = jnp.where(blk_i == idx, -jnp.inf, g)
    sel_ref[0] = jnp.concatenate(rows, axis=0)


def _block_select(q, kmean, n_sel):
    s, nb, n_heads, hd = kmean.shape
    return pl.pallas_call(
        functools.partial(_block_select_kernel, n_sel=n_sel),
        grid=(s,),
        in_specs=[pl.BlockSpec((1, n_heads, hd), lambda si: (si, 0, 0)),
                  pl.BlockSpec((1, nb, n_heads, hd), lambda si: (si, 0, 0, 0))],
        out_specs=pl.BlockSpec((1, n_sel, n_heads), lambda si: (si, 0, 0)),
        out_shape=jax.ShapeDtypeStruct((s, n_sel, n_heads), jnp.int32),
        name="block_select",
    )(q, kmean)


def _moba_step_kernel(pt_ref, sel_ref, q_ref, kn_ref, vn_ref, ck_hbm, cv_hbm, o_ref, kbuf, vbuf,
                      sems, *, n_sel, npp, npages, scale):
    si, ns = pl.program_id(0), pl.num_programs(0)
    n_heads = q_ref.shape[1]
    page = ck_hbm.shape[1]
    group = n_heads // 2

    def gather(seq, grp):
        copies = []
        for hl in range(group):
            h = grp * group + hl
            for r in range(n_sel):
                blk = sel_ref[(seq * n_sel + r) * n_heads + h]
                for p in range(npp):
                    pg = pt_ref[seq * npages + blk * npp + p]
                    rows = pl.ds((r * npp + p) * page, page)
                    copies.append(pltpu.make_async_copy(
                        ck_hbm.at[pg, :, h, :], kbuf.at[grp, hl, rows, :], sems.at[0, grp]))
                    copies.append(pltpu.make_async_copy(
                        cv_hbm.at[pg, :, h, :], vbuf.at[grp, hl, rows, :], sems.at[1, grp]))
        return copies

    def attend(grp):
        for hl in range(group):
            h = grp * group + hl
            q = q_ref[0, h:h + 1, :]
            s = jnp.sum(kbuf[grp, hl] * q, axis=-1, keepdims=True) * scale
            s_own = jnp.sum(kn_ref[0, h:h + 1, :] * q, axis=-1, keepdims=True) * scale
            m = jnp.maximum(jnp.max(s, axis=0, keepdims=True), s_own)
            p = jnp.exp(s - m)
            p_own = jnp.exp(s_own - m)
            l = jnp.sum(p, axis=0, keepdims=True) + p_own
            o = jnp.sum(p * vbuf[grp, hl], axis=0, keepdims=True) + p_own * vn_ref[0, h:h + 1, :]
            o_ref[0, h:h + 1, :] = (o / l).astype(o_ref.dtype)

    @pl.when(si == 0)
    def _():
        for grp in range(2):
            for cp in gather(si, grp):
                cp.start()

    for grp in range(2):
        for cp in gather(si, grp):
            cp.wait()
        attend(grp)

        @pl.when(si + 1 < ns)
        def _():
            for cp in gather(si + 1, grp):
                cp.start()


def _moba_step(q, k_new, v_new, cache_k, cache_v, page_table, sel):
    s, n_heads, hd = q.shape
    _, page, _, _ = cache_k.shape
    npages = page_table.shape[1]
    n_sel = sel.shape[1]
    npp = MOBA_BLOCK // page
    assert n_heads % 2 == 0
    keys = n_sel * MOBA_BLOCK
    row = pl.BlockSpec((1, n_heads, hd), lambda si, pt, sl: (si, 0, 0))
    grid_spec = pltpu.PrefetchScalarGridSpec(
        num_scalar_prefetch=2,
        grid=(s,),
        in_specs=[row, row, row, pl.BlockSpec(memory_space=pl.ANY),
                  pl.BlockSpec(memory_space=pl.ANY)],
        out_specs=row,
        scratch_shapes=[pltpu.VMEM((2, n_heads // 2, keys, hd), cache_k.dtype),
                        pltpu.VMEM((2, n_heads // 2, keys, hd), cache_v.dtype),
                        pltpu.SemaphoreType.DMA((2, 2))],
    )
    scratch_bytes = 2 * _nbytes((n_heads, keys, hd), cache_k.dtype)
    out = pl.pallas_call(
        functools.partial(_moba_step_kernel, n_sel=n_sel, npp=npp, npages=npages, scale=hd ** -0.5),
        grid_spec=grid_spec,
        out_shape=jax.ShapeDtypeStruct((s, n_heads, hd), F32),
        compiler_params=_params(("arbitrary",), 4 * _nbytes((n_heads, hd), F32), scratch_bytes),
        name="moba_step",
    )(page_table.reshape(-1), sel.reshape(-1), q, k_new, v_new, cache_k, cache_v)
    return out.reshape(s, n_heads * hd)


def _swiglu_block(hp, hs, normed, w_gate, w_up, w_down, layer, tag):
    (xp, r_p), (xs, r_s) = normed
    (ap,), (as_,) = _matmul(xp, xs, [(w_gate, layer), (w_up, layer)], _ep_swiglu, [BF16],
                            tm=1024, tn=512, row_scale=(r_p, r_s), name=f"ffn_gate_up{tag}")
    wd = [(w_down, layer)]
    (pp,), (ps,) = _matmul(ap, as_, wd, _ep_identity, [F32], tm=1024, tn=512, k_part=(0, 2),
                           name=f"ffn_down_a{tag}")
    (hp,), (hs,) = _matmul(ap, as_, wd, _ep_residual_partial, [F32], [pp, hp], [ps, hs],
                           tm=1024, tn=512, k_part=(1, 2), name=f"ffn_down_b{tag}")
    return hp, hs


def kernel(x_prompt, x_sample, cache_k, cache_v, page_table, state_conv, state_h, mixer_norm, lru_w_x, lru_w_y, lru_conv_w, lru_conv_b, lru_w_a, lru_b_a, lru_w_i, lru_b_i, lru_lambda, lru_w_out, kv_norm, w_k, w_v, w_q, w_o, ffn_norm, w_gate, w_up, w_down, final_norm):
    bp, tp, d = x_prompt.shape
    bs, ts, _ = x_sample.shape
    n_heads, hd = cache_k.shape[2], cache_k.shape[3]
    page = cache_k.shape[1]
    past_len = page_table.shape[1] * page
    depth = mixer_norm.shape[0]
    n_a = lru_w_x.shape[0]
    assert ts == 1, "the decode group advances one position per step"
    assert MOBA_BLOCK % page == 0 and past_len % MOBA_BLOCK == 0
    n_past = past_len // MOBA_BLOCK
    n_sel_s = min(MOBA_TOP_K, n_past)
    assert n_past >= n_sel_s and 0 < n_a < depth
    n_blocks = bs * n_past
    lru_blocks = n_blocks // 4

    hp = x_prompt.reshape(bp * tp, d)
    hs = x_sample.reshape(bs * ts, d)
    conv_p, conv_s, hl_p, hl_s = [], [], [], []
    k_p = v_p = k_s = v_s = kmean_lru = kmean_s = None
    (xp,) = _rmsnorm(hp, [mixer_norm[0]], [BF16], "mixer_norm_p0")
    (xs,) = _rmsnorm(hs, [mixer_norm[0]], [BF16], "mixer_norm_s0")

    for layer in range(depth):
        if layer < n_a:
            a = layer
            lru = (lru_conv_w[a], lru_conv_b[a], lru_w_a[a], lru_b_a[a], lru_w_i[a], lru_b_i[a],
                   lru_lambda[a])
            cwid = lru_conv_w.shape[1]
            (xr_p, gate_p), (xr_s, gate_s) = _matmul(
                xp, xs, [(lru_w_x, a), (lru_w_y, a)], _ep_x_gelu, [F32, F32], tm=1024, tn=512,
                name=f"lru_in{layer}")
            w = xr_p.shape[1]
            xr3 = xr_p.reshape(bp, tp, w)
            hg_p, hlast_p, km = _rglru_seq(
                xr3, gate_p.reshape(bp, tp, w), jnp.zeros((bp, cwid - 1, w), F32),
                jnp.zeros((bp, w), F32), *lru, cache_k, page_table,
                lru_blocks if layer == 0 else 1, first_pos_zero=True)
            if layer == 0:
                kmean_lru = km
            hg_s, hlast_s, nbuf = _rglru_step(xr_s, gate_s, state_conv[a], state_h[a], *lru,
                                              first_pos_zero=(past_len == 0))
            (hp,), (hs,), normed = _matmul(
                hg_p.reshape(bp * tp, w), hg_s, [(lru_w_out, a)], _ep_residual, [F32], [hp], [hs],
                tm=1024, tn=512, gain=ffn_norm[layer], name=f"lru_out{layer}")
            conv_p.append(xr3[:, tp - (cwid - 1):, :])
            conv_s.append(nbuf)
            hl_p.append(hlast_p)
            hl_s.append(hlast_s)
        else:
            b = layer - n_a
            (q_p,), (q_s,) = _matmul(xp, xs, [(w_q, b)], _ep_identity, [F32], tm=1024, tn=512,
                                     name=f"attn_q{layer}")
            first = lru_blocks if layer == n_a else 0
            o_p, km = _moba_seq(q_p.reshape(bp, tp, d), k_p.reshape(bp, tp, d),
                                v_p.reshape(bp, tp, d), n_heads, cache_k, page_table, first,
                                n_blocks - first if layer == n_a else 1)
            if layer == n_a:
                kmean_s = jnp.concatenate([kmean_lru, km], axis=0).reshape(bs, n_past, n_heads, hd)
            q3 = q_s.reshape(bs, n_heads, hd)
            sel = _block_select(q3, kmean_s, n_sel_s)
            o_s = _moba_step(q3, k_s.reshape(bs, n_heads, hd), v_s.reshape(bs, n_heads, hd),
                             cache_k, cache_v, page_table, sel)
            (hp,), (hs,), normed = _matmul(
                o_p.reshape(bp * tp, d), o_s.astype(BF16), [(w_o, b)], _ep_residual, [F32], [hp],
                [hs], tm=1024, tn=512, gain=ffn_norm[layer], name=f"attn_o{layer}")
            k_p, v_p, normed = lax.optimization_barrier((k_p, v_p, normed))

        hp, hs = _swiglu_block(hp, hs, normed, w_gate, w_up, w_down, layer, str(layer))

        gains = []
        if layer == n_a - 1:
            gains.append(kv_norm)
        gains.append(mixer_norm[layer + 1] if layer + 1 < depth else final_norm)
        outs_p = _rmsnorm(hp, gains, [BF16] * (len(gains) - 1) + [BF16 if layer + 1 < depth else F32],
                          f"stream_norm_p{layer}")
        outs_s = _rmsnorm(hs, gains, [BF16] * (len(gains) - 1) + [BF16 if layer + 1 < depth else F32],
                          f"stream_norm_s{layer}")
        if layer == n_a - 1:
            (k_p, v_p), (k_s, v_s) = _matmul(
                outs_p[0], outs_s[0],
                [(w_k.reshape(1, *w_k.shape), 0), (w_v.reshape(1, *w_v.shape), 0)],
                _ep_identity, [F32, F32], tm=1024, tn=512, name="kv")
        xp, xs = outs_p[-1], outs_s[-1]

    y_p, y_s = xp, xs
    return (y_p.reshape(bp, tp, d), y_s.reshape(bs, ts, d),
            k_p.reshape(bp, tp, n_heads, hd), v_p.reshape(bp, tp, n_heads, hd),
            k_s.reshape(bs, ts, n_heads, hd), v_s.reshape(bs, ts, n_heads, hd),
            jnp.stack(conv_p), jnp.stack(conv_s), jnp.stack(hl_p), jnp.stack(hl_s))
```

```python
import functools

import jax
import jax.numpy as jnp
from jax import lax
from jax.experimental import pallas as pl
from jax.experimental.pallas import tpu as pltpu

F32 = jnp.float32
BF16 = jnp.bfloat16

RMS_EPS = 1e-6
LRU_C = 8.0
MOBA_BLOCK = 256
MOBA_TOP_K = 3
LOG2_E = 1.4426950408889634

V7X_VMEM_BYTES = 64 * 1024 * 1024
V7X_SUBLANES = 8
V7X_LANES = 128
_VMEM_INTERNAL_BYTES = 16 * 1024 * 1024


def _pick_tile(dim, target, align):
    best = None
    d = align
    while d <= min(dim, target):
        if dim % d == 0:
            best = d
        d += align
    return best if best is not None else dim


def _nbytes(shape, dtype):
    n = 1
    for s in shape:
        n *= s
    return n * jnp.dtype(dtype).itemsize


def _params(semantics, block_bytes, scratch_bytes=0):
    limit = 2 * block_bytes + scratch_bytes + _VMEM_INTERNAL_BYTES
    limit = min(max(limit, 16 * 1024 * 1024), V7X_VMEM_BYTES - 4 * 1024 * 1024)
    return pltpu.CompilerParams(dimension_semantics=semantics, vmem_limit_bytes=int(limit))


def _rmsnorm_kernel(x_ref, *refs, n_out):
    g_refs, o_refs = refs[:n_out], refs[n_out:]
    x = x_ref[...].astype(F32)
    y = x * lax.rsqrt(jnp.mean(x * x, axis=-1, keepdims=True) + RMS_EPS)
    for g_ref, o_ref in zip(g_refs, o_refs):
        o_ref[...] = (y * g_ref[...].astype(F32)).astype(o_ref.dtype)


def _rmsnorm(x, gains, out_dtypes, name):
    m, d = x.shape
    tm = _pick_tile(m, 256, V7X_SUBLANES)
    n_out = len(gains)
    block_bytes = _nbytes((tm, d), x.dtype) + sum(_nbytes((tm, d), dt) for dt in out_dtypes)
    outs = pl.pallas_call(
        functools.partial(_rmsnorm_kernel, n_out=n_out),
        grid=(m // tm,),
        in_specs=[pl.BlockSpec((tm, d), lambda i: (i, 0))]
        + [pl.BlockSpec((1, d), lambda i: (0, 0))] * n_out,
        out_specs=[pl.BlockSpec((tm, d), lambda i: (i, 0))] * n_out,
        out_shape=[jax.ShapeDtypeStruct((m, d), dt) for dt in out_dtypes],
        compiler_params=_params(("parallel",), block_bytes),
        name=name,
    )(x, *[g.reshape(1, d) for g in gains])
    return outs


def _matmul_kernel(*refs, n_w, n_extra, n_out, epilogue, nj, ck, tm, n_cols, has_gain, has_scale):
    ap_ref, as_ref = refs[0], refs[1]
    pos = 2
    w_refs = refs[pos:pos + n_w]
    pos += n_w
    ep_refs = refs[pos:pos + n_extra]
    pos += n_extra
    es_refs = refs[pos:pos + n_extra]
    pos += n_extra
    g_ref = rp_ref = rs_ref = None
    if has_gain:
        g_ref = refs[pos]
        pos += 1
    if has_scale:
        rp_ref, rs_ref = refs[pos], refs[pos + 1]
        pos += 2
    op_refs = refs[pos:pos + n_out]
    pos += n_out
    os_refs = refs[pos:pos + n_out]
    pos += n_out
    hbp_ref = hbs_ref = rop_ref = ros_ref = ssqp_ref = ssqs_ref = None
    if has_gain:
        hbp_ref, hbs_ref, rop_ref, ros_ref = refs[pos:pos + 4]
        pos += 4
    wb_refs = refs[pos:pos + n_w]
    pos += n_w
    if has_gain:
        ssqp_ref, ssqs_ref = refs[pos], refs[pos + 1]
    s, i = pl.program_id(0), pl.program_id(1)
    staged = s % 2
    ready = (s + 1) % 2
    rows_p = pl.ds(pl.multiple_of(i * tm, tm), tm)
    tn = wb_refs[0].shape[2]

    def stage():
        row = pl.multiple_of(i * ck, ck)
        for w_ref, wb_ref in zip(w_refs, wb_refs):
            wb_ref[staged, pl.ds(row, ck), :] = w_ref[...].astype(BF16)

    def apply(a_ref, e_refs, o_refs, r_ref, hb_ref, ro_ref, ssq_ref, ssq_rows, last):
        a = a_ref[...]
        accs = [jnp.dot(a, wb_ref[ready], preferred_element_type=F32) for wb_ref in wb_refs]
        if has_scale:
            scale = jnp.tile(r_ref[...], (1, tn // V7X_LANES))
            accs = [acc * scale for acc in accs]
        outs = epilogue(*accs, *[e[...] for e in e_refs])
        for o_ref, val in zip(o_refs, outs):
            o_ref[...] = val.astype(o_ref.dtype)
        if has_gain:
            h = outs[0]
            hb_ref[...] = (h * g_ref[...]).astype(hb_ref.dtype)
            sq = h * h
            part = sq[:, 0:V7X_LANES]
            for c in range(1, tn // V7X_LANES):
                part = part + sq[:, c * V7X_LANES:(c + 1) * V7X_LANES]
            total = ssq_ref[ssq_rows, :] + part
            ssq_ref[ssq_rows, :] = total
            if last:
                mean = jnp.sum(total, axis=-1, keepdims=True) / n_cols
                ro_ref[...] = jnp.broadcast_to(lax.rsqrt(mean + RMS_EPS), total.shape)

    def multiply(last):
        apply(ap_ref, ep_refs, op_refs, rp_ref, hbp_ref, rop_ref, ssqp_ref, rows_p, last)

        @pl.when(i == 0)
        def _():
            apply(as_ref, es_refs, os_refs, rs_ref, hbs_ref, ros_ref, ssqs_ref, slice(None), last)

    @pl.when(s == 0)
    def _():
        stage()
        if has_gain:
            ssqp_ref[rows_p, :] = jnp.zeros((tm, V7X_LANES), F32)
            ssqs_ref[...] = jnp.zeros_like(ssqs_ref)

    @pl.when((s > 0) & (s < nj))
    def _():
        stage()
        multiply(False)

    @pl.when(s == nj)
    def _():
        multiply(True)


def _matmul(a_p, a_s, ws, epilogue, out_dtypes, extras_p=(), extras_s=(), *, tm, tn,
            k_part=(0, 1), gain=None, row_scale=None, name):
    m, kdim = a_p.shape
    s_rows = a_s.shape[0]
    n = ws[0][0].shape[2]
    kp, nkp = k_part
    assert kdim % nkp == 0
    tk = kdim // nkp
    tm = _pick_tile(m, tm, V7X_SUBLANES)
    ni = m // tm
    n_w, n_extra, n_out = len(ws), len(extras_p), len(out_dtypes)
    assert len(extras_s) == n_extra
    tn = min(tn, n)
    assert tn % V7X_LANES == 0
    nj = pl.cdiv(n, tn)
    assert tk % ni == 0 and (tk // ni) % (2 * V7X_SUBLANES) == 0
    ck = tk // ni
    has_gain, has_scale = gain is not None, row_scale is not None
    assert not has_gain or n % tn == 0

    def col(s):
        return jnp.maximum(s - 1, 0)

    def row(s, i):
        return jnp.where(s == 0, 0, i)

    def w_spec(layer):
        def index(s, i):
            return (layer, kp * ni + jnp.where(s == nj, ni - 1, i), jnp.minimum(s, nj - 1))
        return pl.BlockSpec((None, ck, tn), index)

    p_tile = pl.BlockSpec((tm, tn), lambda s, i: (row(s, i), col(s)))
    s_tile = pl.BlockSpec((s_rows, tn), lambda s, i: (0, col(s)))
    lanes_s = pl.BlockSpec((s_rows, V7X_LANES), lambda s, i: (0, 0))
    in_specs = ([pl.BlockSpec((tm, tk), lambda s, i: (row(s, i), kp)),
                 pl.BlockSpec((s_rows, tk), lambda s, i: (0, kp))]
                + [w_spec(layer) for _, layer in ws]
                + [p_tile] * n_extra + [s_tile] * n_extra)
    operands = [a_p, a_s, *[w for w, _ in ws], *extras_p, *extras_s]
    out_specs = [p_tile] * n_out + [s_tile] * n_out
    out_shape = ([jax.ShapeDtypeStruct((m, n), dt) for dt in out_dtypes]
                 + [jax.ShapeDtypeStruct((s_rows, n), dt) for dt in out_dtypes])
    scratch = [pltpu.VMEM((2, tk, tn), BF16) for _ in ws]
    scratch_bytes = 2 * n_w * _nbytes((tk, tn), BF16)
    block_bytes = (_nbytes((tm + s_rows, tk), BF16) + n_w * _nbytes((ck, tn), F32)
                   + sum(_nbytes((tm + s_rows, tn), e.dtype) for e in extras_p)
                   + sum(_nbytes((tm + s_rows, tn), dt) for dt in out_dtypes))
    if has_gain:
        in_specs.append(pl.BlockSpec((1, tn), lambda s, i: (0, col(s))))
        operands.append(gain.reshape(1, n).astype(F32))
        out_specs += [p_tile, s_tile,
                      pl.BlockSpec((tm, V7X_LANES), lambda s, i: (jnp.where(s == nj, i, 0), 0)),
                      lanes_s]
        out_shape += [jax.ShapeDtypeStruct((m, n), BF16), jax.ShapeDtypeStruct((s_rows, n), BF16),
                      jax.ShapeDtypeStruct((m, V7X_LANES), F32),
                      jax.ShapeDtypeStruct((s_rows, V7X_LANES), F32)]
        scratch += [pltpu.VMEM((m, V7X_LANES), F32), pltpu.VMEM((s_rows, V7X_LANES), F32)]
        scratch_bytes += _nbytes((m + s_rows, V7X_LANES), F32)
        block_bytes += _nbytes((tm + s_rows, tn), BF16) + _nbytes((tm + s_rows, V7X_LANES), F32)
    if has_scale:
        in_specs += [pl.BlockSpec((tm, V7X_LANES), lambda s, i: (row(s, i), 0)), lanes_s]
        operands += list(row_scale)
        block_bytes += _nbytes((tm + s_rows, V7X_LANES), F32)
    outs = pl.pallas_call(
        functools.partial(_matmul_kernel, n_w=n_w, n_extra=n_extra, n_out=n_out, epilogue=epilogue,
                          nj=nj, ck=ck, tm=tm, n_cols=n, has_gain=has_gain, has_scale=has_scale),
        grid=(nj + 1, ni),
        in_specs=in_specs,
        out_specs=out_specs,
        out_shape=out_shape,
        scratch_shapes=scratch,
        compiler_params=_params(("arbitrary", "arbitrary"), block_bytes, scratch_bytes),
        name=name,
    )(*operands)
    if has_gain:
        hb_p, hb_s, r_p, r_s = outs[2 * n_out:]
        return outs[:n_out], outs[n_out:2 * n_out], ((hb_p, r_p), (hb_s, r_s))
    return outs[:n_out], outs[n_out:]


def _ep_identity(*accs):
    return accs


def _ep_x_gelu(acc_x, acc_y):
    return acc_x, jax.nn.gelu(acc_y)


def _ep_residual(acc, res):
    return (res + acc,)


def _ep_residual_partial(acc, partial, res):
    return (res + (partial + acc),)


def _ep_swiglu(acc_gate, acc_up):
    return (jax.nn.silu(acc_gate) * acc_up,)


def _cache_mean_specs(cache_k, page_table, first, count, steps, step_of):
    _, page, c_heads, c_hd = cache_k.shape
    npp = MOBA_BLOCK // page
    per = -(-count // steps)

    def page_spec(it, p):
        def index(*args):
            *grid_idx, pt = args
            item = first + jnp.minimum(step_of(*grid_idx) * per + it, count - 1)
            return (pt[item * npp + p], 0, 0, 0)
        return pl.BlockSpec((1, page, c_heads, c_hd), index)

    specs = [page_spec(it, p) for it in range(per) for p in range(npp)]
    out_spec = pl.BlockSpec((per, c_heads, c_hd), lambda *args: (step_of(*args[:-1]), 0, 0))
    out_shape = jax.ShapeDtypeStruct((steps * per, c_heads, c_hd), F32)
    return per, specs, out_spec, out_shape


def _reduce_cache_blocks(page_refs, km_ref, per, npp):
    for it in range(per):
        acc = jnp.sum(page_refs[it * npp][0].astype(F32), axis=0)
        for p in range(1, npp):
            acc = acc + jnp.sum(page_refs[it * npp + p][0].astype(F32), axis=0)
        km_ref[it] = acc / MOBA_BLOCK


def _lru_gates(xc, head, wa_ref, ba_ref, wi_ref, bi_ref, lam):
    xb = xc.astype(BF16)
    r = jax.nn.sigmoid(jnp.dot(xb, wa_ref[head], preferred_element_type=F32) + ba_ref[head])
    i = jax.nn.sigmoid(jnp.dot(xb, wi_ref[head], preferred_element_type=F32) + bi_ref[head])
    log_a = LRU_C * r * jax.nn.log_sigmoid(lam)
    a = jnp.exp(log_a)
    mult = jnp.sqrt(-jnp.tanh(log_a) * (a * a + 1.0))
    return a, mult, i


def _rglru_seq_kernel(pt_ref, xr_ref, gate_ref, cbuf_ref, h0_ref, cw_ref, cb_ref, wa_ref, ba_ref,
                      wi_ref, bi_ref, lam_ref, *refs, tt, nh, dh, cwid, first_pos_zero, per, npp):
    n_pg = per * npp
    page_refs = refs[:n_pg]
    hg_ref, hlast_ref, km_ref, xp_scr, a_scr, u_scr, hs_scr, h_scr = refs[n_pg:]
    _reduce_cache_blocks(page_refs, km_ref, per, npp)
    t = pl.program_id(2)
    halo = cwid - 1
    base = V7X_SUBLANES - halo

    @pl.when(t == 0)
    def _():
        xp_scr[base:V7X_SUBLANES, :] = cbuf_ref[0].astype(F32)
        h_scr[...] = h0_ref[0].astype(F32)

    xp_scr[V7X_SUBLANES:V7X_SUBLANES + tt, :] = xr_ref[0]
    xc = cb_ref[...] + cw_ref[0:1, :] * xp_scr[base:base + tt, :]
    for j in range(1, cwid):
        xc = xc + cw_ref[j:j + 1, :] * xp_scr[base + j:base + j + tt, :]
    xp_scr[base:V7X_SUBLANES, :] = xp_scr[base + tt:V7X_SUBLANES + tt, :]

    for hd in range(nh):
        sl = slice(hd * dh, (hd + 1) * dh)
        xh = xc[:, sl]
        a, mult, i = _lru_gates(xh, hd, wa_ref, ba_ref, wi_ref, bi_ref, lam_ref[:, sl])
        if first_pos_zero:
            row = lax.broadcasted_iota(jnp.int32, mult.shape, 0)
            mult = jnp.where((row == 0) & (t == 0), 1.0, mult)
        a_scr[:, sl] = a
        u_scr[:, sl] = mult * i * xh

    def step(s, h):
        h = a_scr[pl.ds(s, 1), :] * h + u_scr[pl.ds(s, 1), :]
        hs_scr[pl.ds(s, 1), :] = h
        return h

    h = lax.fori_loop(0, tt, step, h_scr[...], unroll=8)
    h_scr[...] = h
    hg_ref[0] = (hs_scr[...] * gate_ref[0]).astype(hg_ref.dtype)
    hlast_ref[0] = h


def _rglru_seq(xr, gate, conv_buf, h0, conv_w, conv_b, w_a, b_a, w_i, b_i, lam, cache_k, page_table,
               cache_blocks, *, first_pos_zero):
    b, t, w = xr.shape
    nheads, dh = w_a.shape[0], w_a.shape[1]
    cwid = conv_w.shape[0]
    wc = _pick_tile(w, 2048, dh)
    nh = wc // dh
    tt = _pick_tile(t, 256, V7X_SUBLANES)
    n_c, n_t = w // wc, t // tt
    blk = lambda bi_, ci, ti, *_: (bi_, ti, ci)
    per_bc = lambda bi_, ci, ti, *_: (bi_, 0, ci)
    chan = lambda bi_, ci, ti, *_: (0, ci)
    head = lambda bi_, ci, ti, *_: (ci, 0, 0)
    npp = MOBA_BLOCK // cache_k.shape[1]
    per, page_specs, km_spec, km_shape = _cache_mean_specs(
        cache_k, page_table, 0, cache_blocks, b * n_c * n_t,
        lambda bi_, ci, ti: (bi_ * n_c + ci) * n_t + ti)
    block_bytes = (2 * _nbytes((tt, wc), F32) + _nbytes((tt, wc), BF16)
                   + 2 * _nbytes((nh, dh, dh), BF16) + 16 * _nbytes((1, wc), F32)
                   + per * npp * _nbytes(cache_k.shape[1:], cache_k.dtype))
    scratch_bytes = 4 * _nbytes((tt + V7X_SUBLANES, wc), F32)
    hg, hlast, kmean = pl.pallas_call(
        functools.partial(_rglru_seq_kernel, tt=tt, nh=nh, dh=dh, cwid=cwid,
                          first_pos_zero=first_pos_zero, per=per, npp=npp),
        grid_spec=pltpu.PrefetchScalarGridSpec(
            num_scalar_prefetch=1,
            grid=(b, n_c, n_t),
            in_specs=[
                pl.BlockSpec((1, tt, wc), blk),
                pl.BlockSpec((1, tt, wc), blk),
                pl.BlockSpec((1, cwid - 1, wc), per_bc),
                pl.BlockSpec((1, 1, wc), per_bc),
                pl.BlockSpec((cwid, wc), chan),
                pl.BlockSpec((1, wc), chan),
                pl.BlockSpec((nh, dh, dh), head),
                pl.BlockSpec((nh, 1, dh), head),
                pl.BlockSpec((nh, dh, dh), head),
                pl.BlockSpec((nh, 1, dh), head),
                pl.BlockSpec((1, wc), chan),
            ] + page_specs,
            out_specs=[pl.BlockSpec((1, tt, wc), blk), pl.BlockSpec((1, 1, wc), per_bc), km_spec],
            scratch_shapes=[
                pltpu.VMEM((tt + V7X_SUBLANES, wc), F32),
                pltpu.VMEM((tt, wc), F32),
                pltpu.VMEM((tt, wc), F32),
                pltpu.VMEM((tt, wc), F32),
                pltpu.VMEM((1, wc), F32),
            ]),
        out_shape=[jax.ShapeDtypeStruct((b, t, w), BF16), jax.ShapeDtypeStruct((b, 1, w), F32),
                   km_shape],
        compiler_params=_params(("parallel", "parallel", "arbitrary"), block_bytes, scratch_bytes),
        name="rglru_seq",
    )(page_table.reshape(-1), xr, gate, conv_buf, h0.reshape(b, 1, w), conv_w, conv_b.reshape(1, w),
      w_a.astype(BF16), b_a.reshape(nheads, 1, dh), w_i.astype(BF16), b_i.reshape(nheads, 1, dh),
      lam.reshape(1, w), *([cache_k] * (per * npp)))
    return hg, hlast.reshape(b, w), kmean[:cache_blocks]


def _rglru_step_kernel(xr_ref, gate_ref, cbuf_ref, h0_ref, cw_ref, cb_ref, wa_ref, ba_ref, wi_ref,
                       bi_ref, lam_ref, hg_ref, h_ref, nbuf_ref, *, nheads, dh, cwid,
                       first_pos_zero):
    xr = xr_ref[...]
    xc = cb_ref[...] + cw_ref[cwid - 1:cwid, :] * xr
    for j in range(cwid - 1):
        xc = xc + cw_ref[j:j + 1, :] * cbuf_ref[j]
    for j in range(cwid - 2):
        nbuf_ref[j] = cbuf_ref[j + 1]
    nbuf_ref[cwid - 2] = xr
    for hd in range(nheads):
        sl = slice(hd * dh, (hd + 1) * dh)
        xh = xc[:, sl]
        a, mult, i = _lru_gates(xh, hd, wa_ref, ba_ref, wi_ref, bi_ref, lam_ref[:, sl])
        if first_pos_zero:
            mult = jnp.ones_like(mult)
        h = a * h0_ref[:, sl] + mult * i * xh
        h_ref[:, sl] = h
        hg_ref[:, sl] = (h * gate_ref[:, sl]).astype(hg_ref.dtype)


def _rglru_step(xr, gate, conv_buf, h0, conv_w, conv_b, w_a, b_a, w_i, b_i, lam, *, first_pos_zero):
    s, w = xr.shape
    nheads, dh = w_a.shape[0], w_a.shape[1]
    cwid = conv_w.shape[0]
    hg, h, nbuf = pl.pallas_call(
        functools.partial(_rglru_step_kernel, nheads=nheads, dh=dh, cwid=cwid,
                          first_pos_zero=first_pos_zero),
        out_shape=[jax.ShapeDtypeStruct((s, w), BF16), jax.ShapeDtypeStruct((s, w), F32),
                   jax.ShapeDtypeStruct((cwid - 1, s, w), F32)],
        name="rglru_step",
    )(xr, gate, jnp.swapaxes(conv_buf, 0, 1), h0, conv_w, conv_b.reshape(1, w),
      w_a.astype(BF16), b_a.reshape(nheads, 1, dh), w_i.astype(BF16), b_i.reshape(nheads, 1, dh),
      lam.reshape(1, w))
    return hg, h, jnp.swapaxes(nbuf, 0, 1)


def _moba_seq_kernel(pt_ref, q_ref, k_ref, v_ref, *refs, nblk, blk, n_sel, scale, per, npp):
    page_refs, o_ref, km_ref = refs[:per * npp], refs[per * npp], refs[per * npp + 1]
    _reduce_cache_blocks(page_refs, km_ref, per, npp)

    k32 = k_ref[0]
    kb = k32.astype(BF16)
    v_t = v_ref[0].T.astype(BF16)
    nblk_pad = -(-nblk // V7X_SUBLANES) * V7X_SUBLANES
    means = [jnp.mean(k32[b * blk:(b + 1) * blk], axis=0, keepdims=True) for b in range(nblk)]
    means += [jnp.zeros_like(means[0])] * (nblk_pad - nblk)
    kmean = jnp.concatenate(means, axis=0)
    key_i = lax.broadcasted_iota(jnp.int32, (blk, blk), 0)
    qry_i = lax.broadcasted_iota(jnp.int32, (blk, blk), 1)
    causal = key_i <= qry_i
    neg_inf = jnp.float32(-jnp.inf)
    contract_last = (((1,), (1,)), ((), ()))
    exp2_scale = scale * LOG2_E

    for c in range(nblk):
        q32 = q_ref[0, c * blk:(c + 1) * blk, :]
        qb = q32.astype(BF16)
        sels = [None] * c
        if c > n_sel:
            gate = lax.dot_general(kmean, q32, contract_last, precision=lax.Precision.HIGHEST,
                                   preferred_element_type=F32)
            g = [gate[b:b + 1, :] for b in range(c)]
            for b in range(c):
                cnt = jnp.zeros((1, blk), jnp.int32)
                for b2 in range(c):
                    if b2 == b:
                        continue
                    beats = (g[b2] >= g[b]) if b2 < b else (g[b2] > g[b])
                    cnt = cnt + beats.astype(jnp.int32)
                sels[b] = cnt < n_sel
        s_blocks = []
        for b in range(c + 1):
            s = lax.dot_general(kb[b * blk:(b + 1) * blk], qb, contract_last,
                                preferred_element_type=F32)
            if b == c:
                s = jnp.where(causal, s, neg_inf)
            elif sels[b] is not None:
                s = jnp.where(sels[b], s, neg_inf)
            s_blocks.append(s)
        m = jnp.max(s_blocks[c], axis=0, keepdims=True)
        for b in range(c):
            m = jnp.maximum(m, jnp.max(s_blocks[b], axis=0, keepdims=True))
        l = jnp.zeros((1, blk), F32)
        o_t = jnp.zeros((v_t.shape[0], blk), F32)
        for b in range(c + 1):
            p = jnp.exp2((s_blocks[b] - m) * exp2_scale)
            l = l + jnp.sum(p, axis=0, keepdims=True)
            o_t = o_t + jnp.dot(v_t[:, b * blk:(b + 1) * blk], p.astype(BF16),
                                preferred_element_type=F32)
        o_ref[0, c * blk:(c + 1) * blk, :] = (o_t / l).T.astype(o_ref.dtype)


def _moba_seq(q, k, v, n_heads, cache_k, page_table, first_block, cache_blocks):
    b, t, d = q.shape
    hd = d // n_heads
    assert t % MOBA_BLOCK == 0 and hd % V7X_LANES == 0
    nblk = t // MOBA_BLOCK
    npp = MOBA_BLOCK // cache_k.shape[1]
    per, page_specs, km_spec, km_shape = _cache_mean_specs(
        cache_k, page_table, first_block, cache_blocks, b * n_heads,
        lambda bi, hi: bi * n_heads + hi)
    spec = pl.BlockSpec((1, t, hd), lambda bi, hi, pt: (bi, 0, hi))
    grid_spec = pltpu.PrefetchScalarGridSpec(
        num_scalar_prefetch=1,
        grid=(b, n_heads),
        in_specs=[spec, spec, spec] + page_specs,
        out_specs=[spec, km_spec],
    )
    block_bytes = (3 * _nbytes((t, hd), F32) + _nbytes((t, hd), BF16)
                   + per * npp * _nbytes(cache_k.shape[1:], cache_k.dtype)
                   + per * _nbytes(cache_k.shape[2:], F32))
    o, kmean = pl.pallas_call(
        functools.partial(_moba_seq_kernel, nblk=nblk, blk=MOBA_BLOCK,
                          n_sel=min(MOBA_TOP_K, nblk - 1), scale=hd ** -0.5, per=per, npp=npp),
        grid_spec=grid_spec,
        out_shape=[jax.ShapeDtypeStruct((b, t, d), BF16), km_shape],
        compiler_params=_params(("parallel", "parallel"), block_bytes),
        name="moba_seq",
    )(page_table.reshape(-1), q, k, v, *([cache_k] * (per * npp)))
    return o, kmean[:cache_blocks]


def _block_select_kernel(q_ref, km_ref, sel_ref, *, n_sel):
    q = q_ref[0]
    km = km_ref[0]
    nb = km.shape[0]
    g = jnp.sum(km * q[None], axis=-1)
    blk_i = lax.broadcasted_iota(jnp.int32, g.shape, 0)
    rows = []
    for _ in range(n_sel):
        m = jnp.max(g, axis=0, keepdims=True)
        idx = jnp.min(jnp.where(g == m, blk_i, nb), axis=0, keepdims=True)
        rows.append(idx)
        g = jnp.where(blk_i == idx, -jnp.inf, g)
    sel_ref[0] = jnp.concatenate(rows, axis=0)


def _block_select(q, kmean, n_sel):
    s, nb, n_heads, hd = kmean.shape
    return pl.pallas_call(
        functools.partial(_block_select_kernel, n_sel=n_sel),
        grid=(s,),
        in_specs=[pl.BlockSpec((1, n_heads, hd), lambda si: (si, 0, 0)),
                  pl.BlockSpec((1, nb, n_heads, hd), lambda si: (si, 0, 0, 0))],
        out_specs=pl.BlockSpec((1, n_sel, n_heads), lambda si: (si, 0, 0)),
        out_shape=jax.ShapeDtypeStruct((s, n_sel, n_heads), jnp.int32),
        name="block_select",
    )(q, kmean)


def _moba_step_kernel(pt_ref, sel_ref, q_ref, kn_ref, vn_ref, ck_hbm, cv_hbm, o_ref, kbuf, vbuf,
                      sems, *, n_sel, npp, npages, scale):
    si, ns = pl.program_id(0), pl.num_programs(0)
    n_heads = q_ref.shape[1]
    page = ck_hbm.shape[1]
    group = n_heads // 2

    def gather(seq, grp):
        copies = []
        for hl in range(group):
            h = grp * group + hl
            for r in range(n_sel):
                blk = sel_ref[(seq * n_sel + r) * n_heads + h]
                for p in range(npp):
                    pg = pt_ref[seq * npages + blk * npp + p]
                    rows = pl.ds((r * npp + p) * page, page)
                    copies.append(pltpu.make_async_copy(
                        ck_hbm.at[pg, :, h, :], kbuf.at[grp, hl, rows, :], sems.at[0, grp]))
                    copies.append(pltpu.make_async_copy(
                        cv_hbm.at[pg, :, h, :], vbuf.at[grp, hl, rows, :], sems.at[1, grp]))
        return copies

    def attend(grp):
        for hl in range(group):
            h = grp * group + hl
            q = q_ref[0, h:h + 1, :]
            s = jnp.sum(kbuf[grp, hl] * q, axis=-1, keepdims=True) * scale
            s_own = jnp.sum(kn_ref[0, h:h + 1, :] * q, axis=-1, keepdims=True) * scale
            m = jnp.maximum(jnp.max(s, axis=0, keepdims=True), s_own)
            p = jnp.exp(s - m)
            p_own = jnp.exp(s_own - m)
            l = jnp.sum(p, axis=0, keepdims=True) + p_own
            o = jnp.sum(p * vbuf[grp, hl], axis=0, keepdims=True) + p_own * vn_ref[0, h:h + 1, :]
            o_ref[0, h:h + 1, :] = (o / l).astype(o_ref.dtype)

    @pl.when(si == 0)
    def _():
        for grp in range(2):
            for cp in gather(si, grp):
                cp.start()

    for grp in range(2):
        for cp in gather(si, grp):
            cp.wait()
        attend(grp)

        @pl.when(si + 1 < ns)
        def _():
            for cp in gather(si + 1, grp):
                cp.start()


def _moba_step(q, k_new, v_new, cache_k, cache_v, page_table, sel):
    s, n_heads, hd = q.shape
    _, page, _, _ = cache_k.shape
    npages = page_table.shape[1]
    n_sel = sel.shape[1]
    npp = MOBA_BLOCK // page
    assert n_heads % 2 == 0
    keys = n_sel * MOBA_BLOCK
    row = pl.BlockSpec((1, n_heads, hd), lambda si, pt, sl: (si, 0, 0))
    grid_spec = pltpu.PrefetchScalarGridSpec(
        num_scalar_prefetch=2,
        grid=(s,),
        in_specs=[row, row, row, pl.BlockSpec(memory_space=pl.ANY),
                  pl.BlockSpec(memory_space=pl.ANY)],
        out_specs=row,
        scratch_shapes=[pltpu.VMEM((2, n_heads // 2, keys, hd), cache_k.dtype),
                        pltpu.VMEM((2, n_heads // 2, keys, hd), cache_v.dtype),
                        pltpu.SemaphoreType.DMA((2, 2))],
    )
    scratch_bytes = 2 * _nbytes((n_heads, keys, hd), cache_k.dtype)
    out = pl.pallas_call(
        functools.partial(_moba_step_kernel, n_sel=n_sel, npp=npp, npages=npages, scale=hd ** -0.5),
        grid_spec=grid_spec,
        out_shape=jax.ShapeDtypeStruct((s, n_heads, hd), F32),
        compiler_params=_params(("arbitrary",), 4 * _nbytes((n_heads, hd), F32), scratch_bytes),
        name="moba_step",
    )(page_table.reshape(-1), sel.reshape(-1), q, k_new, v_new, cache_k, cache_v)
    return out.reshape(s, n_heads * hd)


def _swiglu_block(hp, hs, normed, w_gate, w_up, w_down, layer, tag):
    (xp, r_p), (xs, r_s) = normed
    (ap,), (as_,) = _matmul(xp, xs, [(w_gate, layer), (w_up, layer)], _ep_swiglu, [BF16],
                            tm=1024, tn=512, row_scale=(r_p, r_s), name=f"ffn_gate_up{tag}")
    wd = [(w_down, layer)]
    (pp,), (ps,) = _matmul(ap, as_, wd, _ep_identity, [F32], tm=1024, tn=512, k_part=(0, 2),
                           name=f"ffn_down_a{tag}")
    (hp,), (hs,) = _matmul(ap, as_, wd, _ep_residual_partial, [F32], [pp, hp], [ps, hs],
                           tm=1024, tn=512, k_part=(1, 2), name=f"ffn_down_b{tag}")
    return hp, hs


def kernel(x_prompt, x_sample, cache_k, cache_v, page_table, state_conv, state_h, mixer_norm, lru_w_x, lru_w_y, lru_conv_w, lru_conv_b, lru_w_a, lru_b_a, lru_w_i, lru_b_i, lru_lambda, lru_w_out, kv_norm, w_k, w_v, w_q, w_o, ffn_norm, w_gate, w_up, w_down, final_norm):
    bp, tp, d = x_prompt.shape
    bs, ts, _ = x_sample.shape
    n_heads, hd = cache_k.shape[2], cache_k.shape[3]
    page = cache_k.shape[1]
    past_len = page_table.shape[1] * page
    depth = mixer_norm.shape[0]
    n_a = lru_w_x.shape[0]
    assert ts == 1, "the decode group advances one position per step"
    assert MOBA_BLOCK % page == 0 and past_len % MOBA_BLOCK == 0
    n_past = past_len // MOBA_BLOCK
    n_sel_s = min(MOBA_TOP_K, n_past)
    assert n_past >= n_sel_s and 0 < n_a < depth
    n_blocks = bs * n_past
    lru_blocks = n_blocks // 4

    hp = x_prompt.reshape(bp * tp, d)
    hs = x_sample.reshape(bs * ts, d)
    conv_p, conv_s, hl_p, hl_s = [], [], [], []
    k_p = v_p = k_s = v_s = kmean_lru = kmean_s = None
    (xp,) = _rmsnorm(hp, [mixer_norm[0]], [BF16], "mixer_norm_p0")
    (xs,) = _rmsnorm(hs, [mixer_norm[0]], [BF16], "mixer_norm_s0")

    for layer in range(depth):
        if layer < n_a:
            a = layer
            lru = (lru_conv_w[a], lru_conv_b[a], lru_w_a[a], lru_b_a[a], lru_w_i[a], lru_b_i[a],
                   lru_lambda[a])
            cwid = lru_conv_w.shape[1]
            (xr_p, gate_p), (xr_s, gate_s) = _matmul(
                xp, xs, [(lru_w_x, a), (lru_w_y, a)], _ep_x_gelu, [F32, F32], tm=1024, tn=512,
                name=f"lru_in{layer}")
            w = xr_p.shape[1]
            xr3 = xr_p.reshape(bp, tp, w)
            hg_p, hlast_p, km = _rglru_seq(
                xr3, gate_p.reshape(bp, tp, w), jnp.zeros((bp, cwid - 1, w), F32),
                jnp.zeros((bp, w), F32), *lru, cache_k, page_table,
                lru_blocks if layer == 0 else 1, first_pos_zero=True)
            if layer == 0:
                kmean_lru = km
            hg_s, hlast_s, nbuf = _rglru_step(xr_s, gate_s, state_conv[a], state_h[a], *lru,
                                              first_pos_zero=(past_len == 0))
            (hp,), (hs,), normed = _matmul(
                hg_p.reshape(bp * tp, w), hg_s, [(lru_w_out, a)], _ep_residual, [F32], [hp], [hs],
                tm=1024, tn=512, gain=ffn_norm[layer], name=f"lru_out{layer}")
            conv_p.append(xr3[:, tp - (cwid - 1):, :])
            conv_s.append(nbuf)
            hl_p.append(hlast_p)
            hl_s.append(hlast_s)
        else:
            b = layer - n_a
            (q_p,), (q_s,) = _matmul(xp, xs, [(w_q, b)], _ep_identity, [F32], tm=1024, tn=512,
                                     name=f"attn_q{layer}")
            first = lru_blocks if layer == n_a else 0
            o_p, km = _moba_seq(q_p.reshape(bp, tp, d), k_p.reshape(bp, tp, d),
                                v_p.reshape(bp, tp, d), n_heads, cache_k, page_table, first,
                                n_blocks - first if layer == n_a else 1)
            if layer == n_a:
                kmean_s = jnp.concatenate([kmean_lru, km], axis=0).reshape(bs, n_past, n_heads, hd)
            q3 = q_s.reshape(bs, n_heads, hd)
            sel = _block_select(q3, kmean_s, n_sel_s)
            o_s = _moba_step(q3, k_s.reshape(bs, n_heads, hd), v_s.reshape(bs, n_heads, hd),
                             cache_k, cache_v, page_table, sel)
            (hp,), (hs,), normed = _matmul(
                o_p.reshape(bp * tp, d), o_s.astype(BF16), [(w_o, b)], _ep_residual, [F32], [hp],
                [hs], tm=1024, tn=512, gain=ffn_norm[layer], name=f"attn_o{layer}")
            k_p, v_p, normed = lax.optimization_barrier((k_p, v_p, normed))

        hp, hs = _swiglu_block(hp, hs, normed, w_gate, w_up, w_down, layer, str(layer))

        gains = []
        if layer == n_a - 1:
            gains.append(kv_norm)
        gains.append(mixer_norm[layer + 1] if layer + 1 < depth else final_norm)
        outs_p = _rmsnorm(hp, gains, [BF16] * (len(gains) - 1) + [BF16 if layer + 1 < depth else F32],
                          f"stream_norm_p{layer}")
        outs_s = _rmsnorm(hs, gains, [BF16] * (len(gains) - 1) + [BF16 if layer + 1 < depth else F32],
                          f"stream_norm_s{layer}")
        if layer == n_a - 1:
            (k_p, v_p), (k_s, v_s) = _matmul(
                outs_p[0], outs_s[0],
                [(w_k.reshape(1, *w_k.shape), 0), (w_v.reshape(1, *w_v.shape), 0)],
                _ep_identity, [F32, F32], tm=1024, tn=512, name="kv")
        xp, xs = outs_p[-1], outs_s[-1]

    y_p, y_s = xp, xs
    return (y_p.reshape(bp, tp, d), y_s.reshape(bs, ts, d),
            k_p.reshape(bp, tp, n_heads, hd), v_p.reshape(bp, tp, n_heads, hd),
            k_s.reshape(bs, ts, n_heads, hd), v_s.reshape(bs, ts, n_heads, hd),
            jnp.stack(conv_p), jnp.stack(conv_s), jnp.stack(hl_p), jnp.stack(hl_s))
```
